```python
import math
import jax, jax.numpy as jnp
from jax import lax
import numpy as np

D_MODEL = 2048
BATCH = 1
SEQ = 8192
DEPTH = 1

GLA_WIDTH = D_MODEL // 2
GLA_HEADS = 4
GLA_DK = (GLA_WIDTH // 2) // GLA_HEADS
GLA_DV = GLA_WIDTH // GLA_HEADS
GLA_GATE_RANK = 16
GLA_TAU = 16.0
GLA_CHUNK = 64
ATT_WIDTH = D_MODEL - GLA_WIDTH
ATT_HEADS = 8
ATT_HD = ATT_WIDTH // ATT_HEADS
DILATED_BRANCHES = ((128, 1), (512, 4), (2048, 16))
BAND_BLOCK = 128
ROPE_THETA = 10000.0
D_FF = 5632
CONV_WIDTH = 3
DEEPNORM_ALPHA = (2.0 * DEPTH) ** 0.25
DEEPNORM_BETA = (8.0 * DEPTH) ** -0.25
ADA_STD = 0.005
LN_EPS = 1e-5
RMS_EPS = 1e-6
MAX_POS_OFFSET = 1024
IN_SIZES = (GLA_HEADS * GLA_DK, GLA_HEADS * GLA_DK, GLA_WIDTH, GLA_WIDTH, GLA_GATE_RANK,
            ATT_WIDTH, ATT_WIDTH, ATT_WIDTH)
IN_IS_VALUE = (False, False, True, False, False, False, False, True)
IN_COLS = sum(IN_SIZES)

kernel_name = "hymba_gla_longnet_convffn_deepnorm_adaln"


def split_sizes(t, sizes):
    out, start = [], 0
    for n in sizes:
        out.append(t[..., start:start + n])
        start += n
    return out


def layer_norm(x, g, b):
    xf = x.astype(jnp.float32)
    mu = jnp.mean(xf, axis=-1, keepdims=True)
    var = jnp.mean(jnp.square(xf - mu), axis=-1, keepdims=True)
    return ((xf - mu) * lax.rsqrt(var + LN_EPS) * g.astype(jnp.float32) + b.astype(jnp.float32)).astype(x.dtype)


def rotary(t, positions):
    half = t.shape[-1] // 2
    inv_freq = ROPE_THETA ** (-jnp.arange(half, dtype=jnp.float32) / half)
    ang = positions.astype(jnp.float32)[..., None] * inv_freq
    cos = jnp.cos(ang)[:, :, None, :]
    sin = jnp.sin(ang)[:, :, None, :]
    t1 = t[..., :half].astype(jnp.float32)
    t2 = t[..., half:].astype(jnp.float32)
    return jnp.concatenate([t1 * cos - t2 * sin, t2 * cos + t1 * sin], axis=-1).astype(t.dtype)


def gla_chunked(q, k, v, log_a):
    B, S, H, dk = q.shape
    dv = v.shape[-1]
    nc = S // GLA_CHUNK

    def chunks(t):
        return t.reshape(B, nc, GLA_CHUNK, H, t.shape[-1]).transpose(1, 0, 3, 2, 4)

    causal = jnp.tril(jnp.ones((GLA_CHUNK, GLA_CHUNK), dtype=bool))[:, :, None]

    def step(state, inp):
        qc, kc, vc, gc = inp
        b = jnp.cumsum(gc, axis=-2)
        diff = b[..., :, None, :] - b[..., None, :, :]
        decay = jnp.exp(jnp.where(causal, diff, -jnp.inf))
        scores = jnp.einsum('bhtd,bhsd,bhtsd->bhts', qc, kc, decay)
        o = jnp.einsum('bhts,bhse->bhte', scores, vc) \
            + jnp.einsum('bhtd,bhde->bhte', qc * jnp.exp(b), state)
        b_last = b[..., -1:, :]
        state = jnp.swapaxes(jnp.exp(b_last), -1, -2) * state \
            + jnp.einsum('bhsd,bhse->bhde', kc * jnp.exp(b_last - b), vc)
        return state, o

    state0 = jnp.zeros((B, H, dk, dv), jnp.float32)
    _, o = lax.scan(step, state0, (chunks(q), chunks(k), chunks(v), chunks(log_a)))
    return o.transpose(1, 0, 3, 2, 4).reshape(B, S, H, dv)


def banded_causal_attention(q, k, v, span):
    L, hd = q.shape[-2], q.shape[-1]
    nb = -(-L // BAND_BLOCK)
    lp = nb * BAND_BLOCK
    n_prev = -(-span // BAND_BLOCK)
    lead = [(0, 0)] * (q.ndim - 2)
    qb = jnp.pad(q, lead + [(0, lp - L), (0, 0)]).reshape(*q.shape[:-2], nb, BAND_BLOCK, hd)

    def with_history(t):
        t = jnp.pad(t, lead + [(n_prev * BAND_BLOCK, lp - L), (0, 0)])
        t = t.reshape(*t.shape[:-2], nb + n_prev, BAND_BLOCK, hd)
        return jnp.concatenate([t[..., j:j + nb, :, :] for j in range(n_prev + 1)], axis=-2)

    kb, vb = with_history(k), with_history(v)
    kw = (n_prev + 1) * BAND_BLOCK
    i = jnp.arange(BAND_BLOCK)[None, :, None]
    m = jnp.arange(kw)[None, None, :]
    blk = jnp.arange(nb)[:, None, None]
    dist = i + n_prev * BAND_BLOCK - m
    key_pos = (blk - n_prev) * BAND_BLOCK + m
    mask = (dist >= 0) & (dist <= span) & (key_pos >= 0)

    s = jnp.einsum('...nid,...nmd->...nim', qb.astype(jnp.float32), kb.astype(jnp.float32))
    s = jnp.where(mask, s, -jnp.inf)
    lse = jax.nn.logsumexp(s, axis=-1)
    p = jnp.exp(s - lse[..., None])
    o = jnp.einsum('...nim,...nmd->...nid', p, vb.astype(jnp.float32))
    o = o.reshape(*q.shape[:-2], lp, hd)[..., :L, :]
    lse = lse.reshape(*q.shape[:-2], lp)[..., :L]
    return o, lse


def dilated_attention(q, k, v):
    B, H, S, hd = q.shape
    q = q * (hd ** -0.5)
    outs, lses = [], []
    for window, dil in DILATED_BRANCHES:
        def to_sub(t):
            return jnp.swapaxes(t.reshape(B, H, S // dil, dil, hd), 2, 3)
        o, lse = banded_causal_attention(to_sub(q), to_sub(k), to_sub(v), window // dil)
        outs.append(jnp.swapaxes(o, 2, 3).reshape(B, H, S, hd))
        lses.append(jnp.swapaxes(lse, 2, 3).reshape(B, H, S))
    w = jax.nn.softmax(jnp.stack(lses, axis=0), axis=0)
    return jnp.einsum('rbhs,rbhsd->bhsd', w, jnp.stack(outs, axis=0))


def token_mixer(u, positions, w_in, w_gla_gate, b_gla_gate, gla_norm_g, w_out):
    B, S, _ = u.shape
    proj = u @ w_in
    gq, gk, gv, gr, glr, aq, ak, av = split_sizes(proj, IN_SIZES)
    f32 = jnp.float32
    q_g = gq.astype(f32).reshape(B, S, GLA_HEADS, GLA_DK) * (GLA_DK ** -0.5)
    k_g = gk.astype(f32).reshape(B, S, GLA_HEADS, GLA_DK)
    v_g = gv.astype(f32).reshape(B, S, GLA_HEADS, GLA_DV)
    gate_logits = glr.astype(f32) @ w_gla_gate.astype(f32) + b_gla_gate.astype(f32)
    log_a = (jax.nn.log_sigmoid(gate_logits) / GLA_TAU).reshape(B, S, GLA_HEADS, GLA_DK)
    o_g = gla_chunked(q_g, k_g, v_g, log_a)
    o_g = o_g * lax.rsqrt(jnp.mean(jnp.square(o_g), axis=-1, keepdims=True) + RMS_EPS) \
        * gla_norm_g.astype(f32)
    o_g = (o_g.reshape(B, S, GLA_WIDTH) * jax.nn.silu(gr.astype(f32))).astype(u.dtype)
    q_a = rotary(aq.reshape(B, S, ATT_HEADS, ATT_HD), positions)
    k_a = rotary(ak.reshape(B, S, ATT_HEADS, ATT_HD), positions)
    v_a = av.reshape(B, S, ATT_HEADS, ATT_HD)
    o_a = dilated_attention(jnp.swapaxes(q_a, 1, 2), jnp.swapaxes(k_a, 1, 2), jnp.swapaxes(v_a, 1, 2))
    o_a = jnp.swapaxes(o_a, 1, 2).reshape(B, S, ATT_WIDTH).astype(u.dtype)
    return jnp.concatenate([o_g, o_a], axis=-1) @ w_out


def conv_ffn(u, w_up, conv_w, conv_b, w_down):
    S = u.shape[1]
    h = u @ w_up
    hp = jnp.pad(h, ((0, 0), (CONV_WIDTH - 1, 0), (0, 0)))
    y = conv_b
    for j in range(CONV_WIDTH):
        y = y + conv_w[j] * hp[:, j:j + S, :]
    a, g = jnp.split(y, 2, axis=-1)
    return (jax.nn.silu(g) * a) @ w_down


def setup_inputs(seed: int = 0) -> dict:
    key = jax.random.key(seed)
    ks = jax.random.split(key, 20)
    f32 = jnp.float32
    x = jax.random.normal(ks[0], (BATCH, SEQ, D_MODEL), f32)
    c = jax.random.normal(ks[1], (BATCH, D_MODEL), f32)
    positions = (jax.random.randint(ks[2], (BATCH, 1), 0, MAX_POS_OFFSET)
                 + jnp.arange(SEQ)[None, :]).astype(jnp.int32)
    w_ada = jax.random.normal(ks[3], (DEPTH, D_MODEL, 6 * D_MODEL), f32) * ADA_STD
    b_ada = jax.random.normal(ks[4], (DEPTH, 6 * D_MODEL), f32) * 0.01
    col_scale = jnp.concatenate([jnp.full((n,), DEEPNORM_BETA if is_v else 1.0, f32)
                                 for n, is_v in zip(IN_SIZES, IN_IS_VALUE)])
    w_in = jax.random.normal(ks[5], (DEPTH, D_MODEL, IN_COLS), f32) * (D_MODEL ** -0.5) * col_scale
    w_gla_gate = jax.random.normal(ks[6], (DEPTH, GLA_GATE_RANK, GLA_HEADS * GLA_DK), f32) * (GLA_GATE_RANK ** -0.5)
    b_gla_gate = jax.random.normal(ks[7], (DEPTH, GLA_HEADS * GLA_DK), f32) * 0.1
    gla_norm_g = 1.0 + 0.02 * jax.random.normal(ks[8], (DEPTH, GLA_DV), f32)
    w_out = jax.random.normal(ks[9], (DEPTH, D_MODEL, D_MODEL), f32) * (D_MODEL ** -0.5) * DEEPNORM_BETA
    ln1_g = 1.0 + 0.02 * jax.random.normal(ks[10], (DEPTH, D_MODEL), f32)
    ln1_b = 0.02 * jax.random.normal(ks[11], (DEPTH, D_MODEL), f32)
    w_up = jax.random.normal(ks[12], (DEPTH, D_MODEL, 2 * D_FF), f32) * (D_MODEL ** -0.5) * DEEPNORM_BETA
    conv_w = jax.random.normal(ks[13], (DEPTH, CONV_WIDTH, 2 * D_FF), f32) * (CONV_WIDTH ** -0.5)
    conv_b = 0.01 * jax.random.normal(ks[14], (DEPTH, 2 * D_FF), f32)
    w_down = jax.random.normal(ks[15], (DEPTH, D_FF, D_MODEL), f32) * (D_FF ** -0.5) * DEEPNORM_BETA
    ln2_g = 1.0 + 0.02 * jax.random.normal(ks[16], (DEPTH, D_MODEL), f32)
    ln2_b = 0.02 * jax.random.normal(ks[17], (DEPTH, D_MODEL), f32)
    return {"x": x, "c": c, "positions": positions, "w_ada": w_ada, "b_ada": b_ada,
            "w_in": w_in, "w_gla_gate": w_gla_gate, "b_gla_gate": b_gla_gate,
            "gla_norm_g": gla_norm_g, "w_out": w_out, "ln1_g": ln1_g, "ln1_b": ln1_b,
            "w_up": w_up, "conv_w": conv_w, "conv_b": conv_b, "w_down": w_down,
            "ln2_g": ln2_g, "ln2_b": ln2_b}


def reference(x, c, positions, w_ada, b_ada, w_in, w_gla_gate, b_gla_gate, gla_norm_g, w_out,
              ln1_g, ln1_b, w_up, conv_w, conv_b, w_down, ln2_g, ln2_b):
    for layer in range(DEPTH):
        mod = jax.nn.silu(c) @ w_ada[layer] + b_ada[layer]
        sh1, sc1, g1, sh2, sc2, g2 = [m[:, None, :] for m in jnp.split(mod, 6, axis=-1)]
        u = x * (1.0 + sc1) + sh1
        y = token_mixer(u, positions, w_in[layer], w_gla_gate[layer], b_gla_gate[layer],
                        gla_norm_g[layer], w_out[layer])
        x = layer_norm(DEEPNORM_ALPHA * x + (1.0 + g1) * y, ln1_g[layer], ln1_b[layer])
        u = x * (1.0 + sc2) + sh2
        y = conv_ffn(u, w_up[layer], conv_w[layer], conv_b[layer], w_down[layer])
        x = layer_norm(DEEPNORM_ALPHA * x + (1.0 + g2) * y, ln2_g[layer], ln2_b[layer])
    return x
```

```python
import functools

import jax
import jax.numpy as jnp
from jax import lax
from jax.experimental import pallas as pl
from jax.experimental.pallas import tpu as pltpu

F32 = jnp.float32
BF16 = jnp.bfloat16

D_MODEL = 2048
GLA_HEADS = 4
GLA_DK = 128
GLA_DV = 256
GLA_WIDTH = GLA_HEADS * GLA_DV
GLA_QK = GLA_HEADS * GLA_DK
GLA_RANK = 16
GLA_TAU = 16.0
ATT_HEADS = 8
ATT_HD = 128
ATT_WIDTH = ATT_HEADS * ATT_HD
ATT_SPAN = 128
ATT_DILATIONS = (1, 4, 16)
D_FF = 5632
ROPE_THETA = 10000.0
DEEPNORM_ALPHA = 2.0 ** 0.25
LN_EPS = 1e-5
RMS_EPS = 1e-6

LANES = 128
GLA_CHUNK = 128
GLA_LEVELS = (64, 32, 16, 8)
GLA_DIAG = 8
ATT_SUPER = ATT_SPAN * max(ATT_DILATIONS)
NEG_BIG = -1e30
VMEM_LIMIT = 56 * 1024 * 1024

COL_GQ, COL_GK, COL_GV, COL_GR = 0, 512, 1024, 2048
COL_AQ, COL_AK, COL_AV = 3072, 4096, 5120
PROJ_COLS = 6144


def _dot(a, b):
    return jnp.dot(a, b, preferred_element_type=F32)


def _dot_nt(a, b):
    return lax.dot_general(a, b, (((1,), (1,)), ((), ())), preferred_element_type=F32)


def _dot_tn(a, b):
    return lax.dot_general(a, b, (((0,), (0,)), ((), ())), preferred_element_type=F32)


def _split_bf16(a):
    hi = a.astype(BF16)
    lo = (a - hi.astype(F32)).astype(BF16)
    return hi, lo


def _params(*sem):
    return pltpu.CompilerParams(dimension_semantics=sem, vmem_limit_bytes=VMEM_LIMIT)


def _ada_kernel(c_ref, w_ref, b_ref, o_ref):
    cc = c_ref[...]
    s = cc / (1.0 + jnp.exp(-cc))
    o_ref[...] = jnp.sum(s * w_ref[...], axis=0, keepdims=True) + b_ref[...]


def _ada(c_col, w_ada, b_ada):
    d, n = w_ada.shape
    tn = 1024
    return pl.pallas_call(
        _ada_kernel,
        grid=(n // tn,),
        in_specs=[pl.BlockSpec((d, 1), lambda j: (0, 0)),
                  pl.BlockSpec((d, tn), lambda j: (0, j)),
                  pl.BlockSpec((1, tn), lambda j: (0, j))],
        out_specs=pl.BlockSpec((1, tn), lambda j: (0, j)),
        out_shape=jax.ShapeDtypeStruct((1, n), F32),
        compiler_params=_params("arbitrary"),
        name="ada_mod",
    )(c_col, w_ada, b_ada)


def _rope_kernel(pos_ref, invf_ref, sign_ref, cos_ref, sin_ref):
    ang = pos_ref[...].astype(F32) * invf_ref[...]
    cos_ref[...] = jnp.cos(ang)
    sin_ref[...] = jnp.sin(ang) * sign_ref[...]


def _rope_tables(pos_col, invf, sign):
    s = pos_col.shape[0]
    tm = min(s, 1024)
    return pl.pallas_call(
        _rope_kernel,
        grid=(s // tm,),
        in_specs=[pl.BlockSpec((tm, 1), lambda i: (i, 0)),
                  pl.BlockSpec((1, LANES), lambda i: (0, 0)),
                  pl.BlockSpec((1, LANES), lambda i: (0, 0))],
        out_specs=[pl.BlockSpec((tm, LANES), lambda i: (i, 0)),
                   pl.BlockSpec((tm, LANES), lambda i: (i, 0))],
        out_shape=[jax.ShapeDtypeStruct((s, LANES), F32)] * 2,
        compiler_params=_params("arbitrary"),
        name="rope_tables",
    )(pos_col, invf, sign)


def _inproj_kernel(x_ref, sc_ref, sh_ref, w_ref, wlr_ref, cos_ref, sin_ref,
                   proj_ref, glr_ref, u_ref, *, tn):
    j = pl.program_id(1)

    @pl.when(j == 0)
    def _():
        u = (x_ref[...] * (1.0 + sc_ref[...]) + sh_ref[...]).astype(BF16)
        u_ref[...] = u
        glr_ref[...] = _dot(u, wlr_ref[...])

    acc = _dot(u_ref[...], w_ref[...])
    col = j * tn
    is_q = (col == COL_GQ) | ((col >= COL_AQ) & (col < COL_AK))
    acc = acc * jnp.where(is_q, ATT_HD ** -0.5, 1.0).astype(F32)
    is_rot = (col >= COL_AQ) & (col < COL_AV)

    @pl.when(is_rot)
    def _():
        cos = cos_ref[...]
        sin = sin_ref[...]
        for h in range(tn // ATT_HD):
            t = acc[:, h * ATT_HD:(h + 1) * ATT_HD]
            proj_ref[:, h * ATT_HD:(h + 1) * ATT_HD] = (
                t * cos + pltpu.roll(t, ATT_HD // 2, axis=1) * sin)

    @pl.when(jnp.logical_not(is_rot))
    def _():
        proj_ref[...] = acc


def _inproj(x, sc, sh, w_main, w_lr, cos, sin):
    s, d = x.shape
    tm = min(s, 1024)
    tn = 512
    return pl.pallas_call(
        functools.partial(_inproj_kernel, tn=tn),
        grid=(s // tm, PROJ_COLS // tn),
        in_specs=[pl.BlockSpec((tm, d), lambda i, j: (i, 0)),
                  pl.BlockSpec((1, d), lambda i, j: (0, 0)),
                  pl.BlockSpec((1, d), lambda i, j: (0, 0)),
                  pl.BlockSpec((d, tn), lambda i, j: (0, j)),
                  pl.BlockSpec((d, LANES), lambda i, j: (0, 0)),
                  pl.BlockSpec((tm, LANES), lambda i, j: (i, 0)),
                  pl.BlockSpec((tm, LANES), lambda i, j: (i, 0))],
        out_specs=[pl.BlockSpec((tm, tn), lambda i, j: (i, j)),
                   pl.BlockSpec((tm, LANES), lambda i, j: (i, 0))],
        out_shape=[jax.ShapeDtypeStruct((s, PROJ_COLS), F32),
                   jax.ShapeDtypeStruct((s, LANES), F32)],
        scratch_shapes=[pltpu.VMEM((tm, d), BF16)],
        compiler_params=_params("arbitrary", "arbitrary"),
        name="in_proj",
    )(x, sc, sh, w_main, w_lr, cos, sin)


def _gla_kernel(q_ref, k_ref, v_ref, gr_ref, glr_ref, wg_ref, bg_ref, gn_ref,
                o_ref, state_ref):
    c = pl.program_id(0)
    C = GLA_CHUNK

    @pl.when(c == 0)
    def _():
        state_ref[...] = jnp.zeros_like(state_ref)

    row = lax.broadcasted_iota(jnp.int32, (C, C), 0)
    col = lax.broadcasted_iota(jnp.int32, (C, C), 1)

    g_hi, g_lo = _split_bf16(glr_ref[...])
    w_hi, w_lo = _split_bf16(wg_ref[...])
    z = _dot(g_hi, w_hi) + _dot(g_hi, w_lo) + _dot(g_lo, w_hi) + bg_ref[...]
    log_a = (jnp.minimum(z, 0.0) - jnp.log1p(jnp.exp(-jnp.abs(z)))) * (1.0 / GLA_TAU)

    tri = (col <= row).astype(BF16)
    a_hi, a_lo = _split_bf16(log_a)
    bcum = _dot(tri, a_hi) + _dot(tri, a_lo)

    nlev = len(GLA_LEVELS)
    r4 = lax.broadcasted_iota(jnp.int32, (nlev * C, C), 0)
    c4 = lax.broadcasted_iota(jnp.int32, (nlev * C, C), 1)
    sel = jnp.zeros((nlev * C, C), F32)
    for li, m in enumerate(GLA_LEVELS):
        t = r4 - li * C
        hit = (t >= 0) & (t < C) & (c4 == (t // (2 * m)) * (2 * m) + m - 1)
        sel = jnp.where(hit, 1.0, sel)
    ref_all = _dot(sel.astype(BF16), bcum.astype(BF16))

    g3 = lax.broadcasted_iota(jnp.int32, (C // GLA_DIAG, GLA_DIAG, C), 0)
    t3 = lax.broadcasted_iota(jnp.int32, (C // GLA_DIAG, GLA_DIAG, C), 1)
    s3 = lax.broadcasted_iota(jnp.int32, (C // GLA_DIAG, GLA_DIAG, C), 2)

    for h in range(GLA_HEADS):
        ks = slice(h * GLA_DK, (h + 1) * GLA_DK)
        vs = slice(h * GLA_DV, (h + 1) * GLA_DV)
        q = q_ref[:, ks]
        k = k_ref[:, ks]
        vb = v_ref[:, vs].astype(BF16)
        b = bcum[:, ks]
        b_last = b[C - 1:C, :]
        state = state_ref[h]

        o = _dot((q * jnp.exp(b)).astype(BF16), state.astype(BF16))

        scores = jnp.zeros((C, C), F32)
        for li, m in enumerate(GLA_LEVELS):
            ref = ref_all[li * C:(li + 1) * C, ks]
            upper = (row & m) != 0
            qm = q * jnp.exp(jnp.where(upper, b - ref, NEG_BIG))
            km = k * jnp.exp(jnp.where(upper, NEG_BIG, ref - b))
            part = _dot_nt(qm.astype(BF16), km.astype(BF16))
            same = (row // (2 * m)) == (col // (2 * m))
            scores = scores + jnp.where(same, part, 0.0)

        q3 = q.reshape(C // GLA_DIAG, GLA_DIAG, GLA_DK)
        k3 = k.reshape(C // GLA_DIAG, GLA_DIAG, GLA_DK)
        b3 = b.reshape(C // GLA_DIAG, GLA_DIAG, GLA_DK)
        diag = jnp.zeros((C // GLA_DIAG, GLA_DIAG, C), F32)
        for jj in range(GLA_DIAG):
            kj = k3[:, jj:jj + 1, :]
            bj = b3[:, jj:jj + 1, :]
            term = q3 * kj * jnp.exp(jnp.minimum(b3 - bj, 0.0))
            colsum = jnp.sum(term, axis=-1, keepdims=True)
            hit = (s3 == g3 * GLA_DIAG + jj) & (t3 >= jj)
            diag = diag + jnp.where(hit, colsum, 0.0)
        scores = scores + diag.reshape(C, C)

        o = o + _dot(scores.astype(BF16), vb)

        kl = (k * jnp.exp(b_last - b)).astype(BF16)
        decay_col = jnp.transpose(jnp.broadcast_to(jnp.exp(b_last), (C, GLA_DK)))
        decay = jnp.concatenate([decay_col, decay_col], axis=1)
        state_ref[h] = state * decay + _dot_tn(kl, vb)

        rs = lax.rsqrt(jnp.mean(o * o, axis=-1, keepdims=True) + RMS_EPS)
        gate = gr_ref[:, vs]
        gate = gate / (1.0 + jnp.exp(-gate))
        o_ref[:, vs] = (o * rs * gn_ref[...] * gate).astype(o_ref.dtype)


def _gla(proj, glr, wg_pad, bg, gnorm):
    s = proj.shape[0]
    C = GLA_CHUNK
    return pl.pallas_call(
        _gla_kernel,
        grid=(s // C,),
        in_specs=[pl.BlockSpec((C, GLA_QK), lambda c: (c, COL_GQ // GLA_QK)),
                  pl.BlockSpec((C, GLA_QK), lambda c: (c, COL_GK // GLA_QK)),
                  pl.BlockSpec((C, GLA_WIDTH), lambda c: (c, COL_GV // GLA_WIDTH)),
                  pl.BlockSpec((C, GLA_WIDTH), lambda c: (c, COL_GR // GLA_WIDTH)),
                  pl.BlockSpec((C, LANES), lambda c: (c, 0)),
                  pl.BlockSpec((LANES, GLA_QK), lambda c: (0, 0)),
                  pl.BlockSpec((1, GLA_QK), lambda c: (0, 0)),
                  pl.BlockSpec((1, GLA_DV), lambda c: (0, 0))],
        out_specs=pl.BlockSpec((C, GLA_WIDTH), lambda c: (c, 0)),
        out_shape=jax.ShapeDtypeStruct((s, GLA_WIDTH), BF16),
        scratch_shapes=[pltpu.VMEM((GLA_HEADS, GLA_DK, GLA_DV), F32)],
        compiler_params=_params("arbitrary"),
        name="gla",
    )(proj, proj, proj, proj, glr, wg_pad, bg, gnorm)


def _att_block(qs, kp, kc, vp, vc, prev_off, row, col):
    qb = qs.astype(BF16)
    s_p = _dot_nt(qb, kp.astype(BF16))
    s_c = _dot_nt(qb, kc.astype(BF16))
    s_p = jnp.where(col >= row + prev_off, s_p, NEG_BIG)
    s_c = jnp.where(col <= row, s_c, NEG_BIG)
    m = jnp.maximum(jnp.max(s_p, axis=1, keepdims=True), jnp.max(s_c, axis=1, keepdims=True))
    p_p = jnp.exp(s_p - m)
    p_c = jnp.exp(s_c - m)
    l = jnp.sum(p_p, axis=1, keepdims=True) + jnp.sum(p_c, axis=1, keepdims=True)
    acc = _dot(p_p.astype(BF16), vp.astype(BF16)) + _dot(p_c.astype(BF16), vc.astype(BF16))
    return acc, m, l


def _att_kernel(q_ref, kc_ref, kp_ref, vc_ref, vp_ref, o_ref, acc_ref, m_ref, l_ref):
    n = pl.program_id(1)
    B = ATT_SPAN
    row = lax.broadcasted_iota(jnp.int32, (B, B), 0)
    col = lax.broadcasted_iota(jnp.int32, (B, B), 1)
    first_off = jnp.where(n > 0, 0, B)

    def run(bi, d, q_start, prev_ref, prev_start, cur_start, prev_off):
        rows = lambda start: pl.ds(start, B, stride=d) if d > 1 else pl.ds(start, B)
        kp_src = kp_ref if prev_ref == "prev" else kc_ref
        vp_src = vp_ref if prev_ref == "prev" else vc_ref
        acc, m, l = _att_block(q_ref[rows(q_start), :],
                               kp_src[rows(prev_start), :], kc_ref[rows(cur_start), :],
                               vp_src[rows(prev_start), :], vc_ref[rows(cur_start), :],
                               prev_off, row, col)
        acc_ref[bi, rows(q_start), :] = acc
        m_ref[bi, rows(q_start), :] = jnp.broadcast_to(m, (B, LANES))
        l_ref[bi, rows(q_start), :] = jnp.broadcast_to(l, (B, LANES))

    for bi, d in enumerate(ATT_DILATIONS):
        nblk = ATT_SUPER // (B * d)
        span = B * d

        def first_body(r, carry, bi=bi, d=d, span=span):
            run(bi, d, r, "prev", ATT_SUPER - span + r, r, first_off)
            return carry

        if d == 1:
            first_body(0, 0)
        else:
            lax.fori_loop(0, d, first_body, 0)

        if nblk > 1:
            def rest_body(it, carry, bi=bi, d=d, span=span):
                blk = it // d + 1
                r = it % d
                start = blk * span + r
                if d == 1:
                    start = pl.multiple_of(start, B)
                run(bi, d, start, "cur", start - span, start, 0)
                return carry

            lax.fori_loop(0, (nblk - 1) * d, rest_body, 0)

    m0, m1, m2 = m_ref[0], m_ref[1], m_ref[2]
    mm = jnp.maximum(jnp.maximum(m0, m1), m2)
    c0, c1, c2 = jnp.exp(m0 - mm), jnp.exp(m1 - mm), jnp.exp(m2 - mm)
    num = c0 * acc_ref[0] + c1 * acc_ref[1] + c2 * acc_ref[2]
    den = c0 * l_ref[0] + c1 * l_ref[1] + c2 * l_ref[2]
    o_ref[...] = (num / den).astype(o_ref.dtype)


def _attention(proj):
    s = proj.shape[0]
    T = ATT_SUPER
    qc, kc, vc = COL_AQ // ATT_HD, COL_AK // ATT_HD, COL_AV // ATT_HD
    blk = (T, ATT_HD)
    nb = len(ATT_DILATIONS)
    return pl.pallas_call(
        _att_kernel,
        grid=(ATT_HEADS, s // T),
        in_specs=[pl.BlockSpec(blk, lambda h, n: (n, qc + h)),
                  pl.BlockSpec(blk, lambda h, n: (n, kc + h)),
                  pl.BlockSpec(blk, lambda h, n: (jnp.maximum(n - 1, 0), kc + h)),
                  pl.BlockSpec(blk, lambda h, n: (n, vc + h)),
                  pl.BlockSpec(blk, lambda h, n: (jnp.maximum(n - 1, 0), vc + h))],
        out_specs=pl.BlockSpec(blk, lambda h, n: (n, h)),
        out_shape=jax.ShapeDtypeStruct((s, ATT_WIDTH), BF16),
        scratch_shapes=[pltpu.VMEM((nb, T, LANES), F32)] * 3,
        compiler_params=_params("arbitrary", "arbitrary"),
        name="dilated_attn",
    )(proj, proj, proj, proj, proj)


def _layer_norm(z, g, b):
    mu = jnp.mean(z, axis=-1, keepdims=True)
    zc = z - mu
    var = jnp.mean(zc * zc, axis=-1, keepdims=True)
    return zc * lax.rsqrt(var + LN_EPS) * g + b


def _outproj_kernel(og_ref, oa_ref, wg_ref, wa_ref, x_ref, gate_ref, lg_ref, lb_ref, o_ref):
    y = _dot(og_ref[...], wg_ref[...]) + _dot(oa_ref[...], wa_ref[...])
    z = DEEPNORM_ALPHA * x_ref[...] + (1.0 + gate_ref[...]) * y
    o_ref[...] = _layer_norm(z, lg_ref[...], lb_ref[...])


def _outproj(og, oa, wo_g, wo_a, x, gate, ln_g, ln_b):
    s, d = x.shape
    tm = min(s, 512)
    vec = pl.BlockSpec((1, d), lambda i: (0, 0))
    return pl.pallas_call(
        _outproj_kernel,
        grid=(s // tm,),
        in_specs=[pl.BlockSpec((tm, GLA_WIDTH), lambda i: (i, 0)),
                  pl.BlockSpec((tm, ATT_WIDTH), lambda i: (i, 0)),
                  pl.BlockSpec((GLA_WIDTH, d), lambda i: (0, 0)),
                  pl.BlockSpec((ATT_WIDTH, d), lambda i: (0, 0)),
                  pl.BlockSpec((tm, d), lambda i: (i, 0)),
                  vec, vec, vec],
        out_specs=pl.BlockSpec((tm, d), lambda i: (i, 0)),
        out_shape=jax.ShapeDtypeStruct((s, d), F32),
        compiler_params=_params("arbitrary"),
        name="out_proj_ln",
    )(og, oa, wo_g, wo_a, x, gate, ln_g, ln_b)


def _ffn_kernel(x_ref, sc_ref, sh_ref, gate_ref, wa_ref, wg_ref, cwa_ref, cwg_ref,
                cba_ref, cbg_ref, wd_ref, lg_ref, lb_ref, o_ref,
                u_ref, acc_ref, ha_ref, hg_ref, ta_ref, tg_ref, *, tm, nf):
    i = pl.program_id(0)
    j = pl.program_id(1)
    H = 8

    @pl.when(j == 0)
    def _():
        u_ref[...] = (x_ref[...] * (1.0 + sc_ref[...]) + sh_ref[...]).astype(BF16)
        acc_ref[...] = jnp.zeros_like(acc_ref)

    @pl.when(i == 0)
    def _():
        ta_ref[j] = jnp.zeros(ta_ref.shape[1:], F32)
        tg_ref[j] = jnp.zeros(tg_ref.shape[1:], F32)

    u = u_ref[...]

    def conv(w_ref, cw_ref, cb_ref, h_ref, t_ref):
        h = _dot(u, w_ref[...])
        h_ref[0:H, :] = t_ref[j]
        h_ref[H:H + tm, :] = h
        t_ref[j] = h[tm - H:tm, :]
        cw = cw_ref[...]
        return (cb_ref[...] + cw[0:1, :] * h_ref[H - 2:H - 2 + tm, :]
                + cw[1:2, :] * h_ref[H - 1:H - 1 + tm, :] + cw[2:3, :] * h)

    ya = conv(wa_ref, cwa_ref, cba_ref, ha_ref, ta_ref)
    yg = conv(wg_ref, cwg_ref, cbg_ref, hg_ref, tg_ref)
    act = (yg / (1.0 + jnp.exp(-yg)) * ya).astype(BF16)
    acc_ref[...] += _dot(act, wd_ref[...])

    @pl.when(j == nf - 1)
    def _():
        z = DEEPNORM_ALPHA * x_ref[...] + (1.0 + gate_ref[...]) * acc_ref[...]
        o_ref[...] = _layer_norm(z, lg_ref[...], lb_ref[...])


def _ffn(x, sc, sh, gate, w_up, conv_w, conv_b, w_down, ln_g, ln_b):
    s, d = x.shape
    f = w_down.shape[0]
    tm = min(s, 512)
    fc = 512
    nf = f // fc
    vec = pl.BlockSpec((1, d), lambda i, j: (0, 0))
    return pl.pallas_call(
        functools.partial(_ffn_kernel, tm=tm, nf=nf),
        grid=(s // tm, nf),
        in_specs=[pl.BlockSpec((tm, d), lambda i, j: (i, 0)),
                  vec, vec, vec,
                  pl.BlockSpec((d, fc), lambda i, j: (0, j)),
                  pl.BlockSpec((d, fc), lambda i, j: (0, j + nf)),
                  pl.BlockSpec((3, fc), lambda i, j: (0, j)),
                  pl.BlockSpec((3, fc), lambda i, j: (0, j + nf)),
                  pl.BlockSpec((1, fc), lambda i, j: (0, j)),
                  pl.BlockSpec((1, fc), lambda i, j: (0, j + nf)),
                  pl.BlockSpec((fc, d), lambda i, j: (j, 0)),
                  vec, vec],
        out_specs=pl.BlockSpec((tm, d), lambda i, j: (i, 0)),
        out_shape=jax.ShapeDtypeStruct((s, d), F32),
        scratch_shapes=[pltpu.VMEM((tm, d), BF16),
                        pltpu.VMEM((tm, d), F32),
                        pltpu.VMEM((tm + 8, fc), F32),
                        pltpu.VMEM((tm + 8, fc), F32),
                        pltpu.VMEM((nf, 8, fc), F32),
                        pltpu.VMEM((nf, 8, fc), F32)],
        compiler_params=_params("arbitrary", "arbitrary"),
        name="conv_ffn_ln",
    )(x, sc, sh, gate, w_up, w_up, conv_w, conv_w, conv_b, conv_b, w_down, ln_g, ln_b)


def kernel(x, c, positions, w_ada, b_ada, w_in, w_gla_gate, b_gla_gate, gla_norm_g, w_out,
           ln1_g, ln1_b, w_up, conv_w, conv_b, w_down, ln2_g, ln2_b):
    batch, s, d = x.shape
    half = ATT_HD // 2
    inv_freq = ROPE_THETA ** (-jnp.arange(half, dtype=F32) / half)
    invf = jnp.concatenate([inv_freq, inv_freq]).reshape(1, ATT_HD)
    sign = jnp.concatenate([-jnp.ones((half,), F32), jnp.ones((half,), F32)]).reshape(1, ATT_HD)
    lr0 = COL_GR + GLA_WIDTH
    outs = []
    for bi in range(batch):
        xb = x[bi]
        for layer in range(w_in.shape[0]):
            mod = _ada(c[bi].reshape(d, 1), w_ada[layer], b_ada[layer].reshape(1, -1))
            sh1, sc1, g1, sh2, sc2, g2 = [mod[:, k * d:(k + 1) * d] for k in range(6)]
            wl = w_in[layer]
            w_main = jnp.concatenate([wl[:, :lr0], wl[:, lr0 + GLA_RANK:]], axis=1).astype(BF16)
            w_lr = jnp.pad(wl[:, lr0:lr0 + GLA_RANK], ((0, 0), (0, LANES - GLA_RANK))).astype(BF16)
            wg_pad = jnp.pad(w_gla_gate[layer], ((0, LANES - GLA_RANK), (0, 0)))
            cos, sin = _rope_tables(positions[bi].reshape(s, 1), invf, sign)
            proj, glr = _inproj(xb, sc1, sh1, w_main, w_lr, cos, sin)
            og = _gla(proj, glr, wg_pad, b_gla_gate[layer].reshape(1, -1),
                      gla_norm_g[layer].reshape(1, -1))
            oa = _attention(proj)
            wo = w_out[layer].astype(BF16)
            xb = _outproj(og, oa, wo[:GLA_WIDTH], wo[GLA_WIDTH:], xb, g1,
                          ln1_g[layer].reshape(1, -1), ln1_b[layer].reshape(1, -1))
            xb = _ffn(xb, sc2, sh2, g2, w_up[layer].astype(BF16), conv_w[layer],
                      conv_b[layer].reshape(1, -1), w_down[layer].astype(BF16),
                      ln2_g[layer].reshape(1, -1), ln2_b[layer].reshape(1, -1))
        outs.append(xb)
    return jnp.stack(outs, axis=0)
```

```python
import functools

import jax
import jax.numpy as jnp
from jax import lax
from jax.experimental import pallas as pl
from jax.experimental.pallas import tpu as pltpu

F32 = jnp.float32
BF16 = jnp.bfloat16

D_MODEL = 2048
GLA_HEADS = 4
GLA_DK = 128
GLA_DV = 256
GLA_WIDTH = GLA_HEADS * GLA_DV
GLA_QK = GLA_HEADS * GLA_DK
GLA_RANK = 16
GLA_TAU = 16.0
ATT_HEADS = 8
ATT_HD = 128
ATT_WIDTH = ATT_HEADS * ATT_HD
ATT_SPAN = 128
ATT_DILATIONS = (1, 4, 16)
D_FF = 5632
ROPE_THETA = 10000.0
DEEPNORM_ALPHA = 2.0 ** 0.25
LN_EPS = 1e-5
RMS_EPS = 1e-6

LANES = 128
GLA_CHUNK = 128
GLA_LEVELS = (64, 32, 16, 8)
GLA_DIAG = 8
ATT_SUPER = ATT_SPAN * max(ATT_DILATIONS)
NEG_BIG = -1e30
VMEM_LIMIT = 56 * 1024 * 1024

COL_GQ, COL_GK, COL_GV, COL_GR = 0, 512, 1024, 2048
COL_AQ, COL_AK, COL_AV = 3072, 4096, 5120
PROJ_COLS = 6144


def _dot(a, b):
    return jnp.dot(a, b, preferred_element_type=F32)


def _dot_nt(a, b):
    return lax.dot_general(a, b, (((1,), (1,)), ((), ())), preferred_element_type=F32)


def _dot_tn(a, b):
    return lax.dot_general(a, b, (((0,), (0,)), ((), ())), preferred_element_type=F32)


def _split_bf16(a):
    hi = a.astype(BF16)
    lo = (a - hi.astype(F32)).astype(BF16)
    return hi, lo


def _params(*sem):
    return pltpu.CompilerParams(dimension_semantics=sem, vmem_limit_bytes=VMEM_LIMIT)


def _ada_kernel(c_ref, w_ref, b_ref, o_ref):
    cc = c_ref[...]
    s = cc / (1.0 + jnp.exp(-cc))
    o_ref[...] = jnp.sum(s * w_ref[...], axis=0, keepdims=True) + b_ref[...]


def _ada(c_col, w_ada, b_ada):
    d, n = w_ada.shape
    tn = 1024
    return pl.pallas_call(
        _ada_kernel,
        grid=(n // tn,),
        in_specs=[pl.BlockSpec((d, 1), lambda j: (0, 0)),
                  pl.BlockSpec((d, tn), lambda j: (0, j)),
                  pl.BlockSpec((1, tn), lambda j: (0, j))],
        out_specs=pl.BlockSpec((1, tn), lambda j: (0, j)),
        out_shape=jax.ShapeDtypeStruct((1, n), F32),
        compiler_params=_params("arbitrary"),
        name="ada_mod",
    )(c_col, w_ada, b_ada)


def _rope_kernel(pos_ref, invf_ref, sign_ref, cos_ref, sin_ref):
    ang = pos_ref[...].astype(F32) * invf_ref[...]
    cos_ref[...] = jnp.cos(ang)
    sin_ref[...] = jnp.sin(ang) * sign_ref[...]


def _rope_tables(pos_col, invf, sign):
    s = pos_col.shape[0]
    tm = min(s, 1024)
    return pl.pallas_call(
        _rope_kernel,
        grid=(s // tm,),
        in_specs=[pl.BlockSpec((tm, 1), lambda i: (i, 0)),
                  pl.BlockSpec((1, LANES), lambda i: (0, 0)),
                  pl.BlockSpec((1, LANES), lambda i: (0, 0))],
        out_specs=[pl.BlockSpec((tm, LANES), lambda i: (i, 0)),
                   pl.BlockSpec((tm, LANES), lambda i: (i, 0))],
        out_shape=[jax.ShapeDtypeStruct((s, LANES), F32)] * 2,
        compiler_params=_params("arbitrary"),
        name="rope_tables",
    )(pos_col, invf, sign)


def _inproj_kernel(x_ref, sc_ref, sh_ref, w_ref, wlr_ref, cos_ref, sin_ref,
                   proj_ref, glr_ref, u_ref, *, tn):
    j = pl.program_id(1)

    @pl.when(j == 0)
    def _():
        u = (x_ref[...] * (1.0 + sc_ref[...]) + sh_ref[...]).astype(BF16)
        u_ref[...] = u
        glr_ref[...] = _dot(u, wlr_ref[...])

    acc = _dot(u_ref[...], w_ref[...])
    col = j * tn
    is_q = (col == COL_GQ) | ((col >= COL_AQ) & (col < COL_AK))
    acc = acc * jnp.where(is_q, ATT_HD ** -0.5, 1.0).astype(F32)
    is_rot = (col >= COL_AQ) & (col < COL_AV)

    @pl.when(is_rot)
    def _():
        cos = cos_ref[...]
        sin = sin_ref[...]
        for h in range(tn // ATT_HD):
            t = acc[:, h * ATT_HD:(h + 1) * ATT_HD]
            proj_ref[:, h * ATT_HD:(h + 1) * ATT_HD] = (
                t * cos + pltpu.roll(t, ATT_HD // 2, axis=1) * sin)

    @pl.when(jnp.logical_not(is_rot))
    def _():
        proj_ref[...] = acc


def _inproj(x, sc, sh, w_main, w_lr, cos, sin):
    s, d = x.shape
    tm = min(s, 1024)
    tn = 512
    return pl.pallas_call(
        functools.partial(_inproj_kernel, tn=tn),
        grid=(s // tm, PROJ_COLS // tn),
        in_specs=[pl.BlockSpec((tm, d), lambda i, j: (i, 0)),
                  pl.BlockSpec((1, d), lambda i, j: (0, 0)),
                  pl.BlockSpec((1, d), lambda i, j: (0, 0)),
                  pl.BlockSpec((d, tn), lambda i, j: (0, j)),
                  pl.BlockSpec((d, LANES), lambda i, j: (0, 0)),
                  pl.BlockSpec((tm, LANES), lambda i, j: (i, 0)),
                  pl.BlockSpec((tm, LANES), lambda i, j: (i, 0))],
        out_specs=[pl.BlockSpec((tm, tn), lambda i, j: (i, j)),
                   pl.BlockSpec((tm, LANES), lambda i, j: (i, 0))],
        out_shape=[jax.ShapeDtypeStruct((s, PROJ_COLS), F32),
                   jax.ShapeDtypeStruct((s, LANES), F32)],
        scratch_shapes=[pltpu.VMEM((tm, d), BF16)],
        compiler_params=_params("arbitrary", "arbitrary"),
        name="in_proj",
    )(x, sc, sh, w_main, w_lr, cos, sin)


def _gla_kernel(q_ref, k_ref, v_ref, gr_ref, glr_ref, wg_ref, bg_ref, gn_ref,
                o_ref, state_ref):
    c = pl.program_id(0)
    C = GLA_CHUNK

    @pl.when(c == 0)
    def _():
        state_ref[...] = jnp.zeros_like(state_ref)

    row = lax.broadcasted_iota(jnp.int32, (C, C), 0)
    col = lax.broadcasted_iota(jnp.int32, (C, C), 1)

    g_hi, g_lo = _split_bf16(glr_ref[...])
    w_hi, w_lo = _split_bf16(wg_ref[...])
    z = _dot(g_hi, w_hi) + _dot(g_hi, w_lo) + _dot(g_lo, w_hi) + bg_ref[...]
    log_a = (jnp.minimum(z, 0.0) - jnp.log1p(jnp.exp(-jnp.abs(z)))) * (1.0 / GLA_TAU)

    tri = (col <= row).astype(BF16)
    a_hi, a_lo = _split_bf16(log_a)
    bcum = _dot(tri, a_hi) + _dot(tri, a_lo)

    nlev = len(GLA_LEVELS)
    r4 = lax.broadcasted_iota(jnp.int32, (nlev * C, C), 0)
    c4 = lax.broadcasted_iota(jnp.int32, (nlev * C, C), 1)
    sel = jnp.zeros((nlev * C, C), F32)
    for li, m in enumerate(GLA_LEVELS):
        t = r4 - li * C
        hit = (t >= 0) & (t < C) & (c4 == (t // (2 * m)) * (2 * m) + m - 1)
        sel = jnp.where(hit, 1.0, sel)
    ref_all = _dot(sel.astype(BF16), bcum.astype(BF16))

    g3 = lax.broadcasted_iota(jnp.int32, (C // GLA_DIAG, GLA_DIAG, C), 0)
    t3 = lax.broadcasted_iota(jnp.int32, (C // GLA_DIAG, GLA_DIAG, C), 1)
    s3 = lax.broadcasted_iota(jnp.int32, (C // GLA_DIAG, GLA_DIAG, C), 2)

    for h in range(GLA_HEADS):
        ks = slice(h * GLA_DK, (h + 1) * GLA_DK)
        vs = slice(h * GLA_DV, (h + 1) * GLA_DV)
        q = q_ref[:, ks]
        k = k_ref[:, ks]
        vb = v_ref[:, vs].astype(BF16)
        b = bcum[:, ks]
        b_last = b[C - 1:C, :]
        state = state_ref[h]

        o = _dot((q * jnp.exp(b)).astype(BF16), state.astype(BF16))

        scores = jnp.zeros((C, C), F32)
        for li, m in enumerate(GLA_LEVELS):
            ref = ref_all[li * C:(li + 1) * C, ks]
            upper = (row & m) != 0
            qm = q * jnp.exp(jnp.where(upper, b - ref, NEG_BIG))
            km = k * jnp.exp(jnp.where(upper, NEG_BIG, ref - b))
            part = _dot_nt(qm.astype(BF16), km.astype(BF16))
            same = (row // (2 * m)) == (col // (2 * m))
            scores = scores + jnp.where(same, part, 0.0)

        q3 = q.reshape(C // GLA_DIAG, GLA_DIAG, GLA_DK)
        k3 = k.reshape(C // GLA_DIAG, GLA_DIAG, GLA_DK)
        b3 = b.reshape(C // GLA_DIAG, GLA_DIAG, GLA_DK)
        diag = jnp.zeros((C // GLA_DIAG, GLA_DIAG, C), F32)
        for jj in range(GLA_DIAG):
            kj = k3[:, jj:jj + 1, :]
            bj = b3[:, jj:jj + 1, :]
            term = q3 * kj * jnp.exp(jnp.minimum(b3 - bj, 0.0))
            colsum = jnp.sum(term, axis=-1, keepdims=True)
            hit = (s3 == g3 * GLA_DIAG + jj) & (t3 >= jj)
            diag = diag + jnp.where(hit, colsum, 0.0)
        scores = scores + diag.reshape(C, C)

        o = o + _dot(scores.astype(BF16), vb)

        kl = (k * jnp.exp(b_last - b)).astype(BF16)
        decay_col = jnp.transpose(jnp.broadcast_to(jnp.exp(b_last), (C, GLA_DK)))
        decay = jnp.concatenate([decay_col, decay_col], axis=1)
        state_ref[h] = state * decay + _dot_tn(kl, vb)

        rs = lax.rsqrt(jnp.mean(o * o, axis=-1, keepdims=True) + RMS_EPS)
        gate = gr_ref[:, vs]
        gate = gate / (1.0 + jnp.exp(-gate))
        o_ref[:, vs] = (o * rs * gn_ref[...] * gate).astype(o_ref.dtype)


def _gla(proj, glr, wg_pad, bg, gnorm):
    s = proj.shape[0]
    C = GLA_CHUNK
    return pl.pallas_call(
        _gla_kernel,
        grid=(s // C,),
        in_specs=[pl.BlockSpec((C, GLA_QK), lambda c: (c, COL_GQ // GLA_QK)),
                  pl.BlockSpec((C, GLA_QK), lambda c: (c, COL_GK // GLA_QK)),
                  pl.BlockSpec((C, GLA_WIDTH), lambda c: (c, COL_GV // GLA_WIDTH)),
                  pl.BlockSpec((C, GLA_WIDTH), lambda c: (c, COL_GR // GLA_WIDTH)),
                  pl.BlockSpec((C, LANES), lambda c: (c, 0)),
                  pl.BlockSpec((LANES, GLA_QK), lambda c: (0, 0)),
                  pl.BlockSpec((1, GLA_QK), lambda c: (0, 0)),
                  pl.BlockSpec((1, GLA_DV), lambda c: (0, 0))],
        out_specs=pl.BlockSpec((C, GLA_WIDTH), lambda c: (c, 0)),
        out_shape=jax.ShapeDtypeStruct((s, GLA_WIDTH), BF16),
        scratch_shapes=[pltpu.VMEM((GLA_HEADS, GLA_DK, GLA_DV), F32)],
        compiler_params=_params("arbitrary"),
        name="gla",
    )(proj, proj, proj, proj, glr, wg_pad, bg, gnorm)


def _att_kernel(q_ref, k_ref, v_ref, o_ref, q1, q4, q16, k1, k4, k16, v1, v4, v16,
                bias_ref, acc_ref, m_ref, l_ref):
    n = pl.program_id(1)
    B = ATT_SPAN
    T = ATT_SUPER
    qd, kd, vd = (q1, q4, q16), (k1, k4, k16), (v1, v4, v16)

    row = lax.broadcasted_iota(jnp.int32, (B, 2 * B), 0)
    col = lax.broadcasted_iota(jnp.int32, (B, 2 * B), 1)
    band = (col >= row) & (col <= row + B)
    bias_ref[0] = jnp.where(band, 0.0, NEG_BIG)
    bias_ref[1] = jnp.where(band & (col >= B), 0.0, NEG_BIG)

    @pl.when(n == 0)
    def _():
        for d, kb, vb in zip(ATT_DILATIONS, kd, vd):
            pitch = T // d + B
            for r in range(d):
                kb[r * pitch:r * pitch + B, :] = jnp.zeros((B, ATT_HD), BF16)
                vb[r * pitch:r * pitch + B, :] = jnp.zeros((B, ATT_HD), BF16)

    @pl.when(n > 0)
    def _():
        for d, kb, vb in zip(ATT_DILATIONS, kd, vd):
            pitch = T // d + B
            for r in range(d):
                kb[r * pitch:r * pitch + B, :] = kb[(r + 1) * pitch - B:(r + 1) * pitch, :]
                vb[r * pitch:r * pitch + B, :] = vb[(r + 1) * pitch - B:(r + 1) * pitch, :]

    for d, qb, kb, vb in zip(ATT_DILATIONS, qd, kd, vd):
        L = T // d
        pitch = L + B
        for r in range(d):
            rows = pl.ds(r, L, stride=d) if d > 1 else pl.ds(0, L)
            qb[r * L:(r + 1) * L, :] = q_ref[rows, :].astype(BF16)
            kb[r * pitch + B:(r + 1) * pitch, :] = k_ref[rows, :].astype(BF16)
            vb[r * pitch + B:(r + 1) * pitch, :] = v_ref[rows, :].astype(BF16)

    ones = jnp.ones((2 * B, ATT_HD), BF16)

    for bi, (d, qb, kb, vb) in enumerate(zip(ATT_DILATIONS, qd, kd, vd)):
        nblk = T // (B * d)
        pitch = T // d + B

        def body(it, carry, bi=bi, d=d, qb=qb, kb=kb, vb=vb, nblk=nblk, pitch=pitch):
            r = it // nblk
            blk = it % nblk
            qs = qb[pl.ds(pl.multiple_of(it * B, B), B), :]
            kstart = pl.multiple_of(r * pitch + blk * B, B)
            kcat = kb[pl.ds(kstart, 2 * B), :]
            vcat = vb[pl.ds(kstart, 2 * B), :]
            no_prev = jnp.where((n == 0) & (blk == 0), 1, 0)
            s = _dot_nt(qs, kcat) + bias_ref[no_prev]
            m = jnp.max(s, axis=1, keepdims=True)
            p = jnp.exp(s - m).astype(BF16)
            pv = _dot(p, jnp.concatenate([vcat, ones], axis=1))
            start = r + blk * B * d
            dst = pl.ds(start, B, stride=d) if d > 1 else pl.ds(pl.multiple_of(start, B), B)
            acc_ref[bi, dst, :] = pv[:, :ATT_HD]
            l_ref[bi, dst, :] = pv[:, ATT_HD:]
            m_ref[bi, dst, :] = jnp.broadcast_to(m, (B, LANES))
            return carry

        lax.fori_loop(0, T // B, body, 0, unroll=8)

    m0, m1, m2 = m_ref[0], m_ref[1], m_ref[2]
    mm = jnp.maximum(jnp.maximum(m0, m1), m2)
    c0, c1, c2 = jnp.exp(m0 - mm), jnp.exp(m1 - mm), jnp.exp(m2 - mm)
    num = c0 * acc_ref[0] + c1 * acc_ref[1] + c2 * acc_ref[2]
    den = c0 * l_ref[0] + c1 * l_ref[1] + c2 * l_ref[2]
    o_ref[...] = (num / den).astype(o_ref.dtype)


def _attention(proj):
    s = proj.shape[0]
    T = ATT_SUPER
    qc, kc, vc = COL_AQ // ATT_HD, COL_AK // ATT_HD, COL_AV // ATT_HD
    blk = (T, ATT_HD)
    nb = len(ATT_DILATIONS)
    q_bufs = [pltpu.VMEM((T, ATT_HD), BF16) for _ in ATT_DILATIONS]
    kv_bufs = [pltpu.VMEM((T + d * ATT_SPAN, ATT_HD), BF16) for d in ATT_DILATIONS]
    return pl.pallas_call(
        _att_kernel,
        grid=(ATT_HEADS, s // T),
        in_specs=[pl.BlockSpec(blk, lambda h, n: (n, qc + h)),
                  pl.BlockSpec(blk, lambda h, n: (n, kc + h)),
                  pl.BlockSpec(blk, lambda h, n: (n, vc + h))],
        out_specs=pl.BlockSpec(blk, lambda h, n: (n, h)),
        out_shape=jax.ShapeDtypeStruct((s, ATT_WIDTH), BF16),
        scratch_shapes=q_bufs + kv_bufs + kv_bufs
        + [pltpu.VMEM((2, ATT_SPAN, 2 * ATT_SPAN), F32)]
        + [pltpu.VMEM((nb, T, LANES), F32)] * 3,
        compiler_params=_params("arbitrary", "arbitrary"),
        name="dilated_attn",
    )(proj, proj, proj)


def _layer_norm(z, g, b):
    mu = jnp.mean(z, axis=-1, keepdims=True)
    zc = z - mu
    var = jnp.mean(zc * zc, axis=-1, keepdims=True)
    return zc * lax.rsqrt(var + LN_EPS) * g + b


def _outproj_kernel(og_ref, oa_ref, wg_ref, wa_ref, x_ref, gate_ref, lg_ref, lb_ref, o_ref):
    y = _dot(og_ref[...], wg_ref[...]) + _dot(oa_ref[...], wa_ref[...])
    z = DEEPNORM_ALPHA * x_ref[...] + (1.0 + gate_ref[...]) * y
    o_ref[...] = _layer_norm(z, lg_ref[...], lb_ref[...])


def _outproj(og, oa, wo_g, wo_a, x, gate, ln_g, ln_b):
    s, d = x.shape
    tm = min(s, 512)
    vec = pl.BlockSpec((1, d), lambda i: (0, 0))
    return pl.pallas_call(
        _outproj_kernel,
        grid=(s // tm,),
        in_specs=[pl.BlockSpec((tm, GLA_WIDTH), lambda i: (i, 0)),
                  pl.BlockSpec((tm, ATT_WIDTH), lambda i: (i, 0)),
                  pl.BlockSpec((GLA_WIDTH, d), lambda i: (0, 0)),
                  pl.BlockSpec((ATT_WIDTH, d), lambda i: (0, 0)),
                  pl.BlockSpec((tm, d), lambda i: (i, 0)),
                  vec, vec, vec],
        out_specs=pl.BlockSpec((tm, d), lambda i: (i, 0)),
        out_shape=jax.ShapeDtypeStruct((s, d), F32),
        compiler_params=_params("arbitrary"),
        name="out_proj_ln",
    )(og, oa, wo_g, wo_a, x, gate, ln_g, ln_b)


def _ffn_kernel(x_ref, sc_ref, sh_ref, gate_ref, wa_ref, wg_ref, cwa_ref, cwg_ref,
                cba_ref, cbg_ref, wd_ref, lg_ref, lb_ref, o_ref,
                u_ref, acc_ref, ha_ref, hg_ref, ta_ref, tg_ref, *, tm, nf):
    i = pl.program_id(0)
    j = pl.program_id(1)
    H = 8

    @pl.when(j == 0)
    def _():
        u_ref[...] = (x_ref[...] * (1.0 + sc_ref[...]) + sh_ref[...]).astype(BF16)
        acc_ref[...] = jnp.zeros_like(acc_ref)

    @pl.when(i == 0)
    def _():
        ta_ref[j] = jnp.zeros(ta_ref.shape[1:], F32)
        tg_ref[j] = jnp.zeros(tg_ref.shape[1:], F32)

    u = u_ref[...]

    def conv(w_ref, cw_ref, cb_ref, h_ref, t_ref):
        h = _dot(u, w_ref[...])
        h_ref[0:H, :] = t_ref[j]
        h_ref[H:H + tm, :] = h
        t_ref[j] = h[tm - H:tm, :]
        cw = cw_ref[...]
        return (cb_ref[...] + cw[0:1, :] * h_ref[H - 2:H - 2 + tm, :]
                + cw[1:2, :] * h_ref[H - 1:H - 1 + tm, :] + cw[2:3, :] * h)

    ya = conv(wa_ref, cwa_ref, cba_ref, ha_ref, ta_ref)
    yg = conv(wg_ref, cwg_ref, cbg_ref, hg_ref, tg_ref)
    act = (yg / (1.0 + jnp.exp(-yg)) * ya).astype(BF16)
    acc_ref[...] += _dot(act, wd_ref[...])

    @pl.when(j == nf - 1)
    def _():
        z = DEEPNORM_ALPHA * x_ref[...] + (1.0 + gate_ref[...]) * acc_ref[...]
        o_ref[...] = _layer_norm(z, lg_ref[...], lb_ref[...])


def _ffn(x, sc, sh, gate, w_up, conv_w, conv_b, w_down, ln_g, ln_b):
    s, d = x.shape
    f = w_down.shape[0]
    tm = min(s, 512)
    fc = 512
    nf = f // fc
    vec = pl.BlockSpec((1, d), lambda i, j: (0, 0))
    return pl.pallas_call(
        functools.partial(_ffn_kernel, tm=tm, nf=nf),
        grid=(s // tm, nf),
        in_specs=[pl.BlockSpec((tm, d), lambda i, j: (i, 0)),
                  vec, vec, vec,
                  pl.BlockSpec((d, fc), lambda i, j: (0, j)),
                  pl.BlockSpec((d, fc), lambda i, j: (0, j + nf)),
                  pl.BlockSpec((3, fc), lambda i, j: (0, j)),
                  pl.BlockSpec((3, fc), lambda i, j: (0, j + nf)),
                  pl.BlockSpec((1, fc), lambda i, j: (0, j)),
                  pl.BlockSpec((1, fc), lambda i, j: (0, j + nf)),
                  pl.BlockSpec((fc, d), lambda i, j: (j, 0)),
                  vec, vec],
        out_specs=pl.BlockSpec((tm, d), lambda i, j: (i, 0)),
        out_shape=jax.ShapeDtypeStruct((s, d), F32),
        scratch_shapes=[pltpu.VMEM((tm, d), BF16),
                        pltpu.VMEM((tm, d), F32),
                        pltpu.VMEM((tm + 8, fc), F32),
                        pltpu.VMEM((tm + 8, fc), F32),
                        pltpu.VMEM((nf, 8, fc), F32),
                        pltpu.VMEM((nf, 8, fc), F32)],
        compiler_params=_params("arbitrary", "arbitrary"),
        name="conv_ffn_ln",
    )(x, sc, sh, gate, w_up, w_up, conv_w, conv_w, conv_b, conv_b, w_down, ln_g, ln_b)


def kernel(x, c, positions, w_ada, b_ada, w_in, w_gla_gate, b_gla_gate, gla_norm_g, w_out,
           ln1_g, ln1_b, w_up, conv_w, conv_b, w_down, ln2_g, ln2_b):
    batch, s, d = x.shape
    half = ATT_HD // 2
    inv_freq = ROPE_THETA ** (-jnp.arange(half, dtype=F32) / half)
    invf = jnp.concatenate([inv_freq, inv_freq]).reshape(1, ATT_HD)
    sign = jnp.concatenate([-jnp.ones((half,), F32), jnp.ones((half,), F32)]).reshape(1, ATT_HD)
    lr0 = COL_GR + GLA_WIDTH
    outs = []
    for bi in range(batch):
        xb = x[bi]
        for layer in range(w_in.shape[0]):
            mod = _ada(c[bi].reshape(d, 1), w_ada[layer], b_ada[layer].reshape(1, -1))
            sh1, sc1, g1, sh2, sc2, g2 = [mod[:, k * d:(k + 1) * d] for k in range(6)]
            wl = w_in[layer]
            w_main = jnp.concatenate([wl[:, :lr0], wl[:, lr0 + GLA_RANK:]], axis=1).astype(BF16)
            w_lr = jnp.pad(wl[:, lr0:lr0 + GLA_RANK], ((0, 0), (0, LANES - GLA_RANK))).astype(BF16)
            wg_pad = jnp.pad(w_gla_gate[layer], ((0, LANES - GLA_RANK), (0, 0)))
            cos, sin = _rope_tables(positions[bi].reshape(s, 1), invf, sign)
            proj, glr = _inproj(xb, sc1, sh1, w_main, w_lr, cos, sin)
            og = _gla(proj, glr, wg_pad, b_gla_gate[layer].reshape(1, -1),
                      gla_norm_g[layer].reshape(1, -1))
            oa = _attention(proj)
            wo = w_out[layer].astype(BF16)
            xb = _outproj(og, oa, wo[:GLA_WIDTH], wo[GLA_WIDTH:], xb, g1,
                          ln1_g[layer].reshape(1, -1), ln1_b[layer].reshape(1, -1))
            xb = _ffn(xb, sc2, sh2, g2, w_up[layer].astype(BF16), conv_w[layer],
                      conv_b[layer].reshape(1, -1), w_down[layer].astype(BF16),
                      ln2_g[layer].reshape(1, -1), ln2_b[layer].reshape(1, -1))
        outs.append(xb)
    return jnp.stack(outs, axis=0)
```

```python
import functools

import jax
import jax.numpy as jnp
from jax import lax
from jax.experimental import pallas as pl
from jax.experimental.pallas import tpu as pltpu

F32 = jnp.float32
BF16 = jnp.bfloat16

D_MODEL = 2048
GLA_HEADS = 4
GLA_DK = 128
GLA_DV = 256
GLA_WIDTH = GLA_HEADS * GLA_DV
GLA_QK = GLA_HEADS * GLA_DK
GLA_RANK = 16
GLA_TAU = 16.0
ATT_HEADS = 8
ATT_HD = 128
ATT_WIDTH = ATT_HEADS * ATT_HD
ATT_SPAN = 128
ATT_DILATIONS = (1, 4, 16)
D_FF = 5632
ROPE_THETA = 10000.0
DEEPNORM_ALPHA = 2.0 ** 0.25
LN_EPS = 1e-5
RMS_EPS = 1e-6

LANES = 128
GLA_CHUNK = 128
GLA_LEVELS = (64, 32, 16, 8)
GLA_DIAG = 8
ATT_SUPER = ATT_SPAN * max(ATT_DILATIONS)
MM_ROWS = 128
NEG_BIG = -1e30
VMEM_LIMIT = 56 * 1024 * 1024

COL_GQ, COL_GK, COL_GV, COL_GR = 0, 512, 1024, 2048
COL_AQ, COL_AK, COL_AV = 3072, 4096, 5120
PROJ_COLS = 6144


def _dot(a, b):
    return jnp.dot(a, b, preferred_element_type=F32)


def _dot_nt(a, b):
    return lax.dot_general(a, b, (((1,), (1,)), ((), ())), preferred_element_type=F32)


def _dot_tn(a, b):
    return lax.dot_general(a, b, (((0,), (0,)), ((), ())), preferred_element_type=F32)


def _split_bf16(a):
    hi = a.astype(BF16)
    lo = (a - hi.astype(F32)).astype(BF16)
    return hi, lo


def _params(*sem, flags=None):
    return pltpu.CompilerParams(dimension_semantics=sem, vmem_limit_bytes=VMEM_LIMIT, flags=flags)


def _ada_kernel(c_ref, w_ref, b_ref, o_ref):
    cc = c_ref[...]
    s = cc / (1.0 + jnp.exp(-cc))
    o_ref[...] = jnp.sum(s * w_ref[...], axis=0, keepdims=True) + b_ref[...]


def _ada(c_col, w_ada, b_ada):
    d, n = w_ada.shape
    tn = 1024
    return pl.pallas_call(
        _ada_kernel,
        grid=(n // tn,),
        in_specs=[pl.BlockSpec((d, 1), lambda j: (0, 0)),
                  pl.BlockSpec((d, tn), lambda j: (0, j)),
                  pl.BlockSpec((1, tn), lambda j: (0, j))],
        out_specs=pl.BlockSpec((1, tn), lambda j: (0, j)),
        out_shape=jax.ShapeDtypeStruct((1, n), F32),
        compiler_params=_params("arbitrary"),
        name="ada_mod",
    )(c_col, w_ada, b_ada)


def _rope_kernel(pos_ref, invf_ref, sign_ref, cos_ref, sin_ref):
    ang = pos_ref[...].astype(F32) * invf_ref[...]
    cos_ref[...] = jnp.cos(ang)
    sin_ref[...] = jnp.sin(ang) * sign_ref[...]


def _rope_tables(pos_col, invf, sign):
    s = pos_col.shape[0]
    tm = min(s, 1024)
    return pl.pallas_call(
        _rope_kernel,
        grid=(s // tm,),
        in_specs=[pl.BlockSpec((tm, 1), lambda i: (i, 0)),
                  pl.BlockSpec((1, LANES), lambda i: (0, 0)),
                  pl.BlockSpec((1, LANES), lambda i: (0, 0))],
        out_specs=[pl.BlockSpec((tm, LANES), lambda i: (i, 0)),
                   pl.BlockSpec((tm, LANES), lambda i: (i, 0))],
        out_shape=[jax.ShapeDtypeStruct((s, LANES), F32)] * 2,
        compiler_params=_params("arbitrary"),
        name="rope_tables",
    )(pos_col, invf, sign)


def _inproj_kernel(x_ref, sc_ref, sh_ref, w_ref, wlr_ref, cos_ref, sin_ref,
                   proj_ref, glr_ref, u_ref, *, tn):
    j = pl.program_id(1)

    @pl.when(j == 0)
    def _():
        u = (x_ref[...] * (1.0 + sc_ref[...]) + sh_ref[...]).astype(BF16)
        u_ref[...] = u
        glr_ref[...] = _dot(u, wlr_ref[...])

    acc = _dot(u_ref[...], w_ref[...])
    col = j * tn
    is_q = (col == COL_GQ) | ((col >= COL_AQ) & (col < COL_AK))
    acc = acc * jnp.where(is_q, ATT_HD ** -0.5, 1.0).astype(F32)
    is_rot = (col >= COL_AQ) & (col < COL_AV)

    @pl.when(is_rot)
    def _():
        cos = cos_ref[...]
        sin = sin_ref[...]
        for h in range(tn // ATT_HD):
            t = acc[:, h * ATT_HD:(h + 1) * ATT_HD]
            proj_ref[:, h * ATT_HD:(h + 1) * ATT_HD] = (
                t * cos + pltpu.roll(t, ATT_HD // 2, axis=1) * sin)

    @pl.when(jnp.logical_not(is_rot))
    def _():
        proj_ref[...] = acc


def _inproj(x, sc, sh, w_main, w_lr, cos, sin):
    s, d = x.shape
    tm = min(s, 1024)
    tn = 512
    return pl.pallas_call(
        functools.partial(_inproj_kernel, tn=tn),
        grid=(s // tm, PROJ_COLS // tn),
        in_specs=[pl.BlockSpec((tm, d), lambda i, j: (i, 0)),
                  pl.BlockSpec((1, d), lambda i, j: (0, 0)),
                  pl.BlockSpec((1, d), lambda i, j: (0, 0)),
                  pl.BlockSpec((d, tn), lambda i, j: (0, j)),
                  pl.BlockSpec((d, LANES), lambda i, j: (0, 0)),
                  pl.BlockSpec((tm, LANES), lambda i, j: (i, 0)),
                  pl.BlockSpec((tm, LANES), lambda i, j: (i, 0))],
        out_specs=[pl.BlockSpec((tm, tn), lambda i, j: (i, j)),
                   pl.BlockSpec((tm, LANES), lambda i, j: (i, 0))],
        out_shape=[jax.ShapeDtypeStruct((s, PROJ_COLS), F32),
                   jax.ShapeDtypeStruct((s, LANES), F32)],
        scratch_shapes=[pltpu.VMEM((tm, d), BF16)],
        compiler_params=_params("arbitrary", "arbitrary"),
        name="in_proj",
    )(x, sc, sh, w_main, w_lr, cos, sin)


def _gla_kernel(q_ref, k_ref, v_ref, gr_ref, glr_ref, wg_ref, bg_ref, gn_ref,
                o_ref, state_ref):
    c = pl.program_id(0)
    C = GLA_CHUNK

    @pl.when(c == 0)
    def _():
        state_ref[...] = jnp.zeros_like(state_ref)

    row = lax.broadcasted_iota(jnp.int32, (C, C), 0)
    col = lax.broadcasted_iota(jnp.int32, (C, C), 1)

    g_hi, g_lo = _split_bf16(glr_ref[...])
    w_hi, w_lo = _split_bf16(wg_ref[...])
    z = _dot(g_hi, w_hi) + _dot(g_hi, w_lo) + _dot(g_lo, w_hi) + bg_ref[...]
    log_a = (jnp.minimum(z, 0.0) - jnp.log1p(jnp.exp(-jnp.abs(z)))) * (1.0 / GLA_TAU)

    tri = (col <= row).astype(BF16)
    a_hi, a_lo = _split_bf16(log_a)
    bcum = _dot(tri, a_hi) + _dot(tri, a_lo)

    nlev = len(GLA_LEVELS)
    r4 = lax.broadcasted_iota(jnp.int32, (nlev * C, C), 0)
    c4 = lax.broadcasted_iota(jnp.int32, (nlev * C, C), 1)
    sel = jnp.zeros((nlev * C, C), F32)
    for li, m in enumerate(GLA_LEVELS):
        t = r4 - li * C
        hit = (t >= 0) & (t < C) & (c4 == (t // (2 * m)) * (2 * m) + m - 1)
        sel = jnp.where(hit, 1.0, sel)
    ref_all = _dot(sel.astype(BF16), bcum.astype(BF16))

    g3 = lax.broadcasted_iota(jnp.int32, (C // GLA_DIAG, GLA_DIAG, C), 0)
    t3 = lax.broadcasted_iota(jnp.int32, (C // GLA_DIAG, GLA_DIAG, C), 1)
    s3 = lax.broadcasted_iota(jnp.int32, (C // GLA_DIAG, GLA_DIAG, C), 2)

    for h in range(GLA_HEADS):
        ks = slice(h * GLA_DK, (h + 1) * GLA_DK)
        vs = slice(h * GLA_DV, (h + 1) * GLA_DV)
        q = q_ref[:, ks]
        k = k_ref[:, ks]
        vb = v_ref[:, vs].astype(BF16)
        b = bcum[:, ks]
        b_last = b[C - 1:C, :]
        state = state_ref[h]

        o = _dot((q * jnp.exp(b)).astype(BF16), state.astype(BF16))

        scores = jnp.zeros((C, C), F32)
        for li, m in enumerate(GLA_LEVELS):
            ref = ref_all[li * C:(li + 1) * C, ks]
            upper = (row & m) != 0
            qm = q * jnp.exp(jnp.where(upper, b - ref, NEG_BIG))
            km = k * jnp.exp(jnp.where(upper, NEG_BIG, ref - b))
            part = _dot_nt(qm.astype(BF16), km.astype(BF16))
            same = (row // (2 * m)) == (col // (2 * m))
            scores = scores + jnp.where(same, part, 0.0)

        q3 = q.reshape(C // GLA_DIAG, GLA_DIAG, GLA_DK)
        k3 = k.reshape(C // GLA_DIAG, GLA_DIAG, GLA_DK)
        b3 = b.reshape(C // GLA_DIAG, GLA_DIAG, GLA_DK)
        diag = jnp.zeros((C // GLA_DIAG, GLA_DIAG, C), F32)
        for jj in range(GLA_DIAG):
            kj = k3[:, jj:jj + 1, :]
            bj = b3[:, jj:jj + 1, :]
            term = q3 * kj * jnp.exp(jnp.minimum(b3 - bj, 0.0))
            colsum = jnp.sum(term, axis=-1, keepdims=True)
            hit = (s3 == g3 * GLA_DIAG + jj) & (t3 >= jj)
            diag = diag + jnp.where(hit, colsum, 0.0)
        scores = scores + diag.reshape(C, C)

        o = o + _dot(scores.astype(BF16), vb)

        kl = (k * jnp.exp(b_last - b)).astype(BF16)
        decay_col = jnp.transpose(jnp.broadcast_to(jnp.exp(b_last), (C, GLA_DK)))
        decay = jnp.concatenate([decay_col, decay_col], axis=1)
        state_ref[h] = state * decay + _dot_tn(kl, vb)

        rs = lax.rsqrt(jnp.mean(o * o, axis=-1, keepdims=True) + RMS_EPS)
        gate = gr_ref[:, vs]
        gate = gate / (1.0 + jnp.exp(-gate))
        o_ref[:, vs] = (o * rs * gn_ref[...] * gate).astype(o_ref.dtype)


def _gla(proj, glr, wg_pad, bg, gnorm):
    s = proj.shape[0]
    C = GLA_CHUNK
    return pl.pallas_call(
        _gla_kernel,
        grid=(s // C,),
        in_specs=[pl.BlockSpec((C, GLA_QK), lambda c: (c, COL_GQ // GLA_QK)),
                  pl.BlockSpec((C, GLA_QK), lambda c: (c, COL_GK // GLA_QK)),
                  pl.BlockSpec((C, GLA_WIDTH), lambda c: (c, COL_GV // GLA_WIDTH)),
                  pl.BlockSpec((C, GLA_WIDTH), lambda c: (c, COL_GR // GLA_WIDTH)),
                  pl.BlockSpec((C, LANES), lambda c: (c, 0)),
                  pl.BlockSpec((LANES, GLA_QK), lambda c: (0, 0)),
                  pl.BlockSpec((1, GLA_QK), lambda c: (0, 0)),
                  pl.BlockSpec((1, GLA_DV), lambda c: (0, 0))],
        out_specs=pl.BlockSpec((C, GLA_WIDTH), lambda c: (c, 0)),
        out_shape=jax.ShapeDtypeStruct((s, GLA_WIDTH), BF16),
        scratch_shapes=[pltpu.VMEM((GLA_HEADS, GLA_DK, GLA_DV), F32)],
        compiler_params=_params("arbitrary"),
        name="gla",
    )(proj, proj, proj, proj, glr, wg_pad, bg, gnorm)


def _att_kernel(q_ref, k_ref, v_ref, o_ref, q1, q4, q16, k1, k4, k16, v1, v4, v16,
                bias_ref, acc_ref, m_ref, l_ref):
    n = pl.program_id(1)
    B = ATT_SPAN
    T = ATT_SUPER
    qd, kd, vd = (q1, q4, q16), (k1, k4, k16), (v1, v4, v16)

    row = lax.broadcasted_iota(jnp.int32, (B, 2 * B), 0)
    col = lax.broadcasted_iota(jnp.int32, (B, 2 * B), 1)
    band = (col >= row) & (col <= row + B)
    bias_ref[0] = jnp.where(band, 0.0, NEG_BIG)
    bias_ref[1] = jnp.where(band & (col >= B), 0.0, NEG_BIG)

    @pl.when(n == 0)
    def _():
        for d, kb, vb in zip(ATT_DILATIONS, kd, vd):
            pitch = T // d + B
            for r in range(d):
                kb[r * pitch:r * pitch + B, :] = jnp.zeros((B, ATT_HD), BF16)
                vb[r * pitch:r * pitch + B, :] = jnp.zeros((B, ATT_HD), BF16)

    @pl.when(n > 0)
    def _():
        for d, kb, vb in zip(ATT_DILATIONS, kd, vd):
            pitch = T // d + B
            for r in range(d):
                kb[r * pitch:r * pitch + B, :] = kb[(r + 1) * pitch - B:(r + 1) * pitch, :]
                vb[r * pitch:r * pitch + B, :] = vb[(r + 1) * pitch - B:(r + 1) * pitch, :]

    for d, qb, kb, vb in zip(ATT_DILATIONS, qd, kd, vd):
        L = T // d
        pitch = L + B
        for r in range(d):
            rows = pl.ds(r, L, stride=d) if d > 1 else pl.ds(0, L)
            qb[r * L:(r + 1) * L, :] = q_ref[rows, :].astype(BF16)
            kb[r * pitch + B:(r + 1) * pitch, :] = k_ref[rows, :].astype(BF16)
            vb[r * pitch + B:(r + 1) * pitch, :] = v_ref[rows, :].astype(BF16)

    ones = jnp.ones((2 * B, ATT_HD), BF16)

    for bi, (d, qb, kb, vb) in enumerate(zip(ATT_DILATIONS, qd, kd, vd)):
        nblk = T // (B * d)
        pitch = T // d + B

        def body(it, carry, bi=bi, d=d, qb=qb, kb=kb, vb=vb, nblk=nblk, pitch=pitch):
            r = it // nblk
            blk = it % nblk
            qs = qb[pl.ds(pl.multiple_of(it * B, B), B), :]
            kstart = pl.multiple_of(r * pitch + blk * B, B)
            kcat = kb[pl.ds(kstart, 2 * B), :]
            vcat = vb[pl.ds(kstart, 2 * B), :]
            no_prev = jnp.where((n == 0) & (blk == 0), 1, 0)
            s = _dot_nt(qs, kcat) + bias_ref[no_prev]
            m = jnp.max(s, axis=1, keepdims=True)
            p = jnp.exp(s - m).astype(BF16)
            pv = _dot(p, jnp.concatenate([vcat, ones], axis=1))
            start = r + blk * B * d
            dst = pl.ds(start, B, stride=d) if d > 1 else pl.ds(pl.multiple_of(start, B), B)
            acc_ref[bi, dst, :] = pv[:, :ATT_HD]
            l_ref[bi, dst, :] = pv[:, ATT_HD:]
            m_ref[bi, dst, :] = jnp.broadcast_to(m, (B, LANES))
            return carry

        lax.fori_loop(0, T // B, body, 0, unroll=8)

    m0, m1, m2 = m_ref[0], m_ref[1], m_ref[2]
    mm = jnp.maximum(jnp.maximum(m0, m1), m2)
    c0, c1, c2 = jnp.exp(m0 - mm), jnp.exp(m1 - mm), jnp.exp(m2 - mm)
    num = c0 * acc_ref[0] + c1 * acc_ref[1] + c2 * acc_ref[2]
    den = c0 * l_ref[0] + c1 * l_ref[1] + c2 * l_ref[2]
    o_ref[...] = (num / den).astype(o_ref.dtype)


def _attention(proj):
    s = proj.shape[0]
    T = ATT_SUPER
    qc, kc, vc = COL_AQ // ATT_HD, COL_AK // ATT_HD, COL_AV // ATT_HD
    blk = (T, ATT_HD)
    nb = len(ATT_DILATIONS)
    q_bufs = [pltpu.VMEM((T, ATT_HD), BF16) for _ in ATT_DILATIONS]
    kv_bufs = [pltpu.VMEM((T + d * ATT_SPAN, ATT_HD), BF16) for d in ATT_DILATIONS]
    return pl.pallas_call(
        _att_kernel,
        grid=(ATT_HEADS, s // T),
        in_specs=[pl.BlockSpec(blk, lambda h, n: (n, qc + h)),
                  pl.BlockSpec(blk, lambda h, n: (n, kc + h)),
                  pl.BlockSpec(blk, lambda h, n: (n, vc + h))],
        out_specs=pl.BlockSpec(blk, lambda h, n: (n, h)),
        out_shape=jax.ShapeDtypeStruct((s, ATT_WIDTH), BF16),
        scratch_shapes=q_bufs + kv_bufs + kv_bufs
        + [pltpu.VMEM((2, ATT_SPAN, 2 * ATT_SPAN), F32)]
        + [pltpu.VMEM((nb, T, LANES), F32)] * 3,
        compiler_params=_params("arbitrary", "arbitrary"),
        name="dilated_attn",
    )(proj, proj, proj)


def _layer_norm(z, g, b):
    mu = jnp.mean(z, axis=-1, keepdims=True)
    zc = z - mu
    var = jnp.mean(zc * zc, axis=-1, keepdims=True)
    return zc * lax.rsqrt(var + LN_EPS) * g + b


def _outproj_kernel(og_ref, oa_ref, wg_ref, wa_ref, x_ref, gate_ref, lg_ref, lb_ref, o_ref):
    tm = o_ref.shape[0]
    for r in range(0, tm, MM_ROWS):
        rows = slice(r, r + MM_ROWS)
        y = _dot(og_ref[rows, :], wg_ref[...]) + _dot(oa_ref[rows, :], wa_ref[...])
        z = DEEPNORM_ALPHA * x_ref[rows, :] + (1.0 + gate_ref[...]) * y
        o_ref[rows, :] = _layer_norm(z, lg_ref[...], lb_ref[...])


def _outproj(og, oa, wo_g, wo_a, x, gate, ln_g, ln_b):
    s, d = x.shape
    tm = min(s, 512)
    vec = pl.BlockSpec((1, d), lambda i: (0, 0))
    return pl.pallas_call(
        _outproj_kernel,
        grid=(s // tm,),
        in_specs=[pl.BlockSpec((tm, GLA_WIDTH), lambda i: (i, 0)),
                  pl.BlockSpec((tm, ATT_WIDTH), lambda i: (i, 0)),
                  pl.BlockSpec((GLA_WIDTH, d), lambda i: (0, 0)),
                  pl.BlockSpec((ATT_WIDTH, d), lambda i: (0, 0)),
                  pl.BlockSpec((tm, d), lambda i: (i, 0)),
                  vec, vec, vec],
        out_specs=pl.BlockSpec((tm, d), lambda i: (i, 0)),
        out_shape=jax.ShapeDtypeStruct((s, d), F32),
        compiler_params=_params("arbitrary"),
        name="out_proj_ln",
    )(og, oa, wo_g, wo_a, x, gate, ln_g, ln_b)


def _ffn_kernel(x_ref, sc_ref, sh_ref, gate_ref, wa_ref, wg_ref, cwa_ref, cwg_ref,
                cba_ref, cbg_ref, wd_ref, lg_ref, lb_ref, o_ref,
                u_ref, acc_ref, ha_ref, hg_ref, ta_ref, tg_ref, *, tm, nf):
    i = pl.program_id(0)
    j = pl.program_id(1)
    H = 8

    @pl.when(j == 0)
    def _():
        u_ref[...] = (x_ref[...] * (1.0 + sc_ref[...]) + sh_ref[...]).astype(BF16)
        acc_ref[...] = jnp.zeros_like(acc_ref)

    @pl.when(i == 0)
    def _():
        ta_ref[j] = jnp.zeros(ta_ref.shape[1:], F32)
        tg_ref[j] = jnp.zeros(tg_ref.shape[1:], F32)

    ha_ref[0:H, :] = ta_ref[j]
    hg_ref[0:H, :] = tg_ref[j]

    R = MM_ROWS

    def up(r):
        u = u_ref[r:r + R, :]
        ha = _dot(u, wa_ref[...])
        hg = _dot(u, wg_ref[...])
        ha_ref[H + r:H + r + R, :] = ha
        hg_ref[H + r:H + r + R, :] = hg
        return ha, hg

    def conv(h, cw_ref, cb_ref, h_ref, r):
        cw = cw_ref[...]
        return (cb_ref[...] + cw[0:1, :] * h_ref[H + r - 2:H + r - 2 + R, :]
                + cw[1:2, :] * h_ref[H + r - 1:H + r - 1 + R, :] + cw[2:3, :] * h)

    nxt = up(0)
    for r in range(0, tm, R):
        ha, hg = nxt
        if r + R < tm:
            nxt = up(r + R)
        ya = conv(ha, cwa_ref, cba_ref, ha_ref, r)
        yg = conv(hg, cwg_ref, cbg_ref, hg_ref, r)
        act = (yg / (1.0 + jnp.exp(-yg)) * ya).astype(BF16)
        acc_ref[r:r + R, :] += _dot(act, wd_ref[...])

    ta_ref[j] = ha_ref[tm:tm + H, :]
    tg_ref[j] = hg_ref[tm:tm + H, :]

    @pl.when(j == nf - 1)
    def _():
        z = DEEPNORM_ALPHA * x_ref[...] + (1.0 + gate_ref[...]) * acc_ref[...]
        o_ref[...] = _layer_norm(z, lg_ref[...], lb_ref[...])


def _ffn(x, sc, sh, gate, w_up, conv_w, conv_b, w_down, ln_g, ln_b):
    s, d = x.shape
    f = w_down.shape[0]
    tm = min(s, 512)
    fc = 512
    nf = f // fc
    vec = pl.BlockSpec((1, d), lambda i, j: (0, 0))
    return pl.pallas_call(
        functools.partial(_ffn_kernel, tm=tm, nf=nf),
        grid=(s // tm, nf),
        in_specs=[pl.BlockSpec((tm, d), lambda i, j: (i, 0)),
                  vec, vec, vec,
                  pl.BlockSpec((d, fc), lambda i, j: (0, j)),
                  pl.BlockSpec((d, fc), lambda i, j: (0, j + nf)),
                  pl.BlockSpec((3, fc), lambda i, j: (0, j)),
                  pl.BlockSpec((3, fc), lambda i, j: (0, j + nf)),
                  pl.BlockSpec((1, fc), lambda i, j: (0, j)),
                  pl.BlockSpec((1, fc), lambda i, j: (0, j + nf)),
                  pl.BlockSpec((fc, d), lambda i, j: (j, 0)),
                  vec, vec],
        out_specs=pl.BlockSpec((tm, d), lambda i, j: (i, 0)),
        out_shape=jax.ShapeDtypeStruct((s, d), F32),
        scratch_shapes=[pltpu.VMEM((tm, d), BF16),
                        pltpu.VMEM((tm, d), F32),
                        pltpu.VMEM((tm + 8, fc), F32),
                        pltpu.VMEM((tm + 8, fc), F32),
                        pltpu.VMEM((nf, 8, fc), F32),
                        pltpu.VMEM((nf, 8, fc), F32)],
        compiler_params=_params("arbitrary", "arbitrary"),
        name="conv_ffn_ln",
    )(x, sc, sh, gate, w_up, w_up, conv_w, conv_w, conv_b, conv_b, w_down, ln_g, ln_b)


def kernel(x, c, positions, w_ada, b_ada, w_in, w_gla_gate, b_gla_gate, gla_norm_g, w_out,
           ln1_g, ln1_b, w_up, conv_w, conv_b, w_down, ln2_g, ln2_b):
    batch, s, d = x.shape
    half = ATT_HD // 2
    inv_freq = ROPE_THETA ** (-jnp.arange(half, dtype=F32) / half)
    invf = jnp.concatenate([inv_freq, inv_freq]).reshape(1, ATT_HD)
    sign = jnp.concatenate([-jnp.ones((half,), F32), jnp.ones((half,), F32)]).reshape(1, ATT_HD)
    lr0 = COL_GR + GLA_WIDTH
    outs = []
    for bi in range(batch):
        xb = x[bi]
        for layer in range(w_in.shape[0]):
            mod = _ada(c[bi].reshape(d, 1), w_ada[layer], b_ada[layer].reshape(1, -1))
            sh1, sc1, g1, sh2, sc2, g2 = [mod[:, k * d:(k + 1) * d] for k in range(6)]
            wl = w_in[layer]
            w_main = jnp.concatenate([wl[:, :lr0].astype(BF16), wl[:, lr0 + GLA_RANK:].astype(BF16)],
                                     axis=1)
            w_lr = jnp.pad(wl[:, lr0:lr0 + GLA_RANK], ((0, 0), (0, LANES - GLA_RANK))).astype(BF16)
            wg_pad = jnp.pad(w_gla_gate[layer], ((0, LANES - GLA_RANK), (0, 0)))
            cos, sin = _rope_tables(positions[bi].reshape(s, 1), invf, sign)
            proj, glr = _inproj(xb, sc1, sh1, w_main, w_lr, cos, sin)
            og = _gla(proj, glr, wg_pad, b_gla_gate[layer].reshape(1, -1),
                      gla_norm_g[layer].reshape(1, -1))
            oa = _attention(proj)
            wo = w_out[layer].astype(BF16)
            xb = _outproj(og, oa, wo[:GLA_WIDTH], wo[GLA_WIDTH:], xb, g1,
                          ln1_g[layer].reshape(1, -1), ln1_b[layer].reshape(1, -1))
            xb = _ffn(xb, sc2, sh2, g2, w_up[layer].astype(BF16), conv_w[layer],
                      conv_b[layer].reshape(1, -1), w_down[layer].astype(BF16),
                      ln2_g[layer].reshape(1, -1), ln2_b[layer].reshape(1, -1))
        outs.append(xb)
    return jnp.stack(outs, axis=0)
```

```python
import functools

import jax
import jax.numpy as jnp
from jax import lax
from jax.experimental import pallas as pl
from jax.experimental.pallas import tpu as pltpu

F32 = jnp.float32
BF16 = jnp.bfloat16

D_MODEL = 2048
GLA_HEADS = 4
GLA_DK = 128
GLA_DV = 256
GLA_WIDTH = GLA_HEADS * GLA_DV
GLA_QK = GLA_HEADS * GLA_DK
GLA_RANK = 16
GLA_TAU = 16.0
ATT_HEADS = 8
ATT_HD = 128
ATT_WIDTH = ATT_HEADS * ATT_HD
ATT_SPAN = 128
ATT_DILATIONS = (1, 4, 16)
D_FF = 5632
ROPE_THETA = 10000.0
DEEPNORM_ALPHA = 2.0 ** 0.25
LN_EPS = 1e-5
RMS_EPS = 1e-6

LANES = 128
GLA_CHUNK = 128
GLA_LEVELS = (64, 32, 16, 8)
GLA_DIAG = 8
ATT_SUPER = ATT_SPAN * max(ATT_DILATIONS)
FFN_ROWS = 512
NEG_BIG = -1e30
VMEM_LIMIT = 56 * 1024 * 1024

COL_GQ, COL_GK, COL_GV, COL_GR = 0, 512, 1024, 2048
COL_AQ, COL_AK, COL_AV = 3072, 4096, 5120
PROJ_COLS = 6144


def _dot(a, b):
    return jnp.dot(a, b, preferred_element_type=F32)


def _dot_nt(a, b):
    return lax.dot_general(a, b, (((1,), (1,)), ((), ())), preferred_element_type=F32)


def _dot_tn(a, b):
    return lax.dot_general(a, b, (((0,), (0,)), ((), ())), preferred_element_type=F32)


def _split_bf16(a):
    hi = a.astype(BF16)
    lo = (a - hi.astype(F32)).astype(BF16)
    return hi, lo


def _params(*sem, flags=None):
    return pltpu.CompilerParams(dimension_semantics=sem, vmem_limit_bytes=VMEM_LIMIT, flags=flags)


def _ada_kernel(c_ref, w_ref, b_ref, o_ref):
    cc = c_ref[...]
    s = cc / (1.0 + jnp.exp(-cc))
    o_ref[...] = jnp.sum(s * w_ref[...], axis=0, keepdims=True) + b_ref[...]


def _ada(c_col, w_ada, b_ada):
    d, n = w_ada.shape
    tn = 1024
    return pl.pallas_call(
        _ada_kernel,
        grid=(n // tn,),
        in_specs=[pl.BlockSpec((d, 1), lambda j: (0, 0)),
                  pl.BlockSpec((d, tn), lambda j: (0, j)),
                  pl.BlockSpec((1, tn), lambda j: (0, j))],
        out_specs=pl.BlockSpec((1, tn), lambda j: (0, j)),
        out_shape=jax.ShapeDtypeStruct((1, n), F32),
        compiler_params=_params("arbitrary"),
        name="ada_mod",
    )(c_col, w_ada, b_ada)


def _rope_kernel(pos_ref, invf_ref, sign_ref, cos_ref, sin_ref):
    ang = pos_ref[...].astype(F32) * invf_ref[...]
    cos_ref[...] = jnp.cos(ang)
    sin_ref[...] = jnp.sin(ang) * sign_ref[...]


def _rope_tables(pos_col, invf, sign):
    s = pos_col.shape[0]
    tm = min(s, 1024)
    return pl.pallas_call(
        _rope_kernel,
        grid=(s // tm,),
        in_specs=[pl.BlockSpec((tm, 1), lambda i: (i, 0)),
                  pl.BlockSpec((1, LANES), lambda i: (0, 0)),
                  pl.BlockSpec((1, LANES), lambda i: (0, 0))],
        out_specs=[pl.BlockSpec((tm, LANES), lambda i: (i, 0)),
                   pl.BlockSpec((tm, LANES), lambda i: (i, 0))],
        out_shape=[jax.ShapeDtypeStruct((s, LANES), F32)] * 2,
        compiler_params=_params("arbitrary"),
        name="rope_tables",
    )(pos_col, invf, sign)


def _inproj_kernel(x_ref, sc_ref, sh_ref, w_ref, wlr_ref, cos_ref, sin_ref,
                   proj_ref, glr_ref, u_ref, *, tn):
    j = pl.program_id(1)

    @pl.when(j == 0)
    def _():
        u = (x_ref[...] * (1.0 + sc_ref[...]) + sh_ref[...]).astype(BF16)
        u_ref[...] = u
        glr_ref[...] = _dot(u, wlr_ref[...])

    acc = _dot(u_ref[...], w_ref[...])
    col = j * tn
    is_q = (col == COL_GQ) | ((col >= COL_AQ) & (col < COL_AK))
    acc = acc * jnp.where(is_q, ATT_HD ** -0.5, 1.0).astype(F32)
    is_rot = (col >= COL_AQ) & (col < COL_AV)

    @pl.when(is_rot)
    def _():
        cos = cos_ref[...]
        sin = sin_ref[...]
        for h in range(tn // ATT_HD):
            t = acc[:, h * ATT_HD:(h + 1) * ATT_HD]
            proj_ref[:, h * ATT_HD:(h + 1) * ATT_HD] = (
                t * cos + pltpu.roll(t, ATT_HD // 2, axis=1) * sin)

    @pl.when(jnp.logical_not(is_rot))
    def _():
        proj_ref[...] = acc


def _inproj(x, sc, sh, w_main, w_lr, cos, sin):
    s, d = x.shape
    tm = min(s, 1024)
    tn = 512
    return pl.pallas_call(
        functools.partial(_inproj_kernel, tn=tn),
        grid=(s // tm, PROJ_COLS // tn),
        in_specs=[pl.BlockSpec((tm, d), lambda i, j: (i, 0)),
                  pl.BlockSpec((1, d), lambda i, j: (0, 0)),
                  pl.BlockSpec((1, d), lambda i, j: (0, 0)),
                  pl.BlockSpec((d, tn), lambda i, j: (0, j)),
                  pl.BlockSpec((d, LANES), lambda i, j: (0, 0)),
                  pl.BlockSpec((tm, LANES), lambda i, j: (i, 0)),
                  pl.BlockSpec((tm, LANES), lambda i, j: (i, 0))],
        out_specs=[pl.BlockSpec((tm, tn), lambda i, j: (i, j)),
                   pl.BlockSpec((tm, LANES), lambda i, j: (i, 0))],
        out_shape=[jax.ShapeDtypeStruct((s, PROJ_COLS), F32),
                   jax.ShapeDtypeStruct((s, LANES), F32)],
        scratch_shapes=[pltpu.VMEM((tm, d), BF16)],
        compiler_params=_params("arbitrary", "arbitrary"),
        name="in_proj",
    )(x, sc, sh, w_main, w_lr, cos, sin)


def _gla_kernel(q_ref, k_ref, v_ref, gr_ref, glr_ref, wg_ref, bg_ref, gn_ref,
                o_ref, state_ref):
    c = pl.program_id(0)
    C = GLA_CHUNK

    @pl.when(c == 0)
    def _():
        state_ref[...] = jnp.zeros_like(state_ref)

    row = lax.broadcasted_iota(jnp.int32, (C, C), 0)
    col = lax.broadcasted_iota(jnp.int32, (C, C), 1)

    g_hi, g_lo = _split_bf16(glr_ref[...])
    w_hi, w_lo = _split_bf16(wg_ref[...])
    z = _dot(g_hi, w_hi) + _dot(g_hi, w_lo) + _dot(g_lo, w_hi) + bg_ref[...]
    log_a = (jnp.minimum(z, 0.0) - jnp.log1p(jnp.exp(-jnp.abs(z)))) * (1.0 / GLA_TAU)

    tri = (col <= row).astype(BF16)
    a_hi, a_lo = _split_bf16(log_a)
    bcum = _dot(tri, a_hi) + _dot(tri, a_lo)

    nlev = len(GLA_LEVELS)
    r4 = lax.broadcasted_iota(jnp.int32, (nlev * C, C), 0)
    c4 = lax.broadcasted_iota(jnp.int32, (nlev * C, C), 1)
    sel = jnp.zeros((nlev * C, C), F32)
    for li, m in enumerate(GLA_LEVELS):
        t = r4 - li * C
        hit = (t >= 0) & (t < C) & (c4 == (t // (2 * m)) * (2 * m) + m - 1)
        sel = jnp.where(hit, 1.0, sel)
    ref_all = _dot(sel.astype(BF16), bcum.astype(BF16))

    g3 = lax.broadcasted_iota(jnp.int32, (C // GLA_DIAG, GLA_DIAG, C), 0)
    t3 = lax.broadcasted_iota(jnp.int32, (C // GLA_DIAG, GLA_DIAG, C), 1)
    s3 = lax.broadcasted_iota(jnp.int32, (C // GLA_DIAG, GLA_DIAG, C), 2)

    for h in range(GLA_HEADS):
        ks = slice(h * GLA_DK, (h + 1) * GLA_DK)
        vs = slice(h * GLA_DV, (h + 1) * GLA_DV)
        q = q_ref[:, ks]
        k = k_ref[:, ks]
        vb = v_ref[:, vs].astype(BF16)
        b = bcum[:, ks]
        b_last = b[C - 1:C, :]
        state = state_ref[h]

        o = _dot((q * jnp.exp(b)).astype(BF16), state.astype(BF16))

        scores = jnp.zeros((C, C), F32)
        for li, m in enumerate(GLA_LEVELS):
            ref = ref_all[li * C:(li + 1) * C, ks]
            upper = (row & m) != 0
            qm = q * jnp.exp(jnp.where(upper, b - ref, NEG_BIG))
            km = k * jnp.exp(jnp.where(upper, NEG_BIG, ref - b))
            part = _dot_nt(qm.astype(BF16), km.astype(BF16))
            same = (row // (2 * m)) == (col // (2 * m))
            scores = scores + jnp.where(same, part, 0.0)

        q3 = q.reshape(C // GLA_DIAG, GLA_DIAG, GLA_DK)
        k3 = k.reshape(C // GLA_DIAG, GLA_DIAG, GLA_DK)
        b3 = b.reshape(C // GLA_DIAG, GLA_DIAG, GLA_DK)
        diag = jnp.zeros((C // GLA_DIAG, GLA_DIAG, C), F32)
        for jj in range(GLA_DIAG):
            kj = k3[:, jj:jj + 1, :]
            bj = b3[:, jj:jj + 1, :]
            term = q3 * kj * jnp.exp(jnp.minimum(b3 - bj, 0.0))
            colsum = jnp.sum(term, axis=-1, keepdims=True)
            hit = (s3 == g3 * GLA_DIAG + jj) & (t3 >= jj)
            diag = diag + jnp.where(hit, colsum, 0.0)
        scores = scores + diag.reshape(C, C)

        o = o + _dot(scores.astype(BF16), vb)

        kl = (k * jnp.exp(b_last - b)).astype(BF16)
        decay_col = jnp.transpose(jnp.broadcast_to(jnp.exp(b_last), (C, GLA_DK)))
        decay = jnp.concatenate([decay_col, decay_col], axis=1)
        state_ref[h] = state * decay + _dot_tn(kl, vb)

        rs = lax.rsqrt(jnp.mean(o * o, axis=-1, keepdims=True) + RMS_EPS)
        gate = gr_ref[:, vs]
        gate = gate / (1.0 + jnp.exp(-gate))
        o_ref[:, vs] = (o * rs * gn_ref[...] * gate).astype(o_ref.dtype)


def _gla(proj, glr, wg_pad, bg, gnorm):
    s = proj.shape[0]
    C = GLA_CHUNK
    return pl.pallas_call(
        _gla_kernel,
        grid=(s // C,),
        in_specs=[pl.BlockSpec((C, GLA_QK), lambda c: (c, COL_GQ // GLA_QK)),
                  pl.BlockSpec((C, GLA_QK), lambda c: (c, COL_GK // GLA_QK)),
                  pl.BlockSpec((C, GLA_WIDTH), lambda c: (c, COL_GV // GLA_WIDTH)),
                  pl.BlockSpec((C, GLA_WIDTH), lambda c: (c, COL_GR // GLA_WIDTH)),
                  pl.BlockSpec((C, LANES), lambda c: (c, 0)),
                  pl.BlockSpec((LANES, GLA_QK), lambda c: (0, 0)),
                  pl.BlockSpec((1, GLA_QK), lambda c: (0, 0)),
                  pl.BlockSpec((1, GLA_DV), lambda c: (0, 0))],
        out_specs=pl.BlockSpec((C, GLA_WIDTH), lambda c: (c, 0)),
        out_shape=jax.ShapeDtypeStruct((s, GLA_WIDTH), BF16),
        scratch_shapes=[pltpu.VMEM((GLA_HEADS, GLA_DK, GLA_DV), F32)],
        compiler_params=_params("arbitrary"),
        name="gla",
    )(proj, proj, proj, proj, glr, wg_pad, bg, gnorm)


def _att_kernel(q_ref, k_ref, v_ref, o_ref, q1, q4, q16, k1, k4, k16, v1, v4, v16,
                bias_ref, acc_ref, m_ref, l_ref):
    n = pl.program_id(1)
    B = ATT_SPAN
    T = ATT_SUPER
    qd, kd, vd = (q1, q4, q16), (k1, k4, k16), (v1, v4, v16)

    row = lax.broadcasted_iota(jnp.int32, (B, 2 * B), 0)
    col = lax.broadcasted_iota(jnp.int32, (B, 2 * B), 1)
    band = (col >= row) & (col <= row + B)
    bias_ref[0] = jnp.where(band, 0.0, NEG_BIG)
    bias_ref[1] = jnp.where(band & (col >= B), 0.0, NEG_BIG)

    @pl.when(n == 0)
    def _():
        for d, kb, vb in zip(ATT_DILATIONS, kd, vd):
            pitch = T // d + B
            for r in range(d):
                kb[r * pitch:r * pitch + B, :] = jnp.zeros((B, ATT_HD), BF16)
                vb[r * pitch:r * pitch + B, :] = jnp.zeros((B, ATT_HD), BF16)

    @pl.when(n > 0)
    def _():
        for d, kb, vb in zip(ATT_DILATIONS, kd, vd):
            pitch = T // d + B
            for r in range(d):
                kb[r * pitch:r * pitch + B, :] = kb[(r + 1) * pitch - B:(r + 1) * pitch, :]
                vb[r * pitch:r * pitch + B, :] = vb[(r + 1) * pitch - B:(r + 1) * pitch, :]

    for d, qb, kb, vb in zip(ATT_DILATIONS, qd, kd, vd):
        L = T // d
        pitch = L + B
        for r in range(d):
            rows = pl.ds(r, L, stride=d) if d > 1 else pl.ds(0, L)
            qb[r * L:(r + 1) * L, :] = q_ref[rows, :].astype(BF16)
            kb[r * pitch + B:(r + 1) * pitch, :] = k_ref[rows, :].astype(BF16)
            vb[r * pitch + B:(r + 1) * pitch, :] = v_ref[rows, :].astype(BF16)

    ones = jnp.ones((2 * B, ATT_HD), BF16)

    for bi, (d, qb, kb, vb) in enumerate(zip(ATT_DILATIONS, qd, kd, vd)):
        nblk = T // (B * d)
        pitch = T // d + B

        def body(it, carry, bi=bi, d=d, qb=qb, kb=kb, vb=vb, nblk=nblk, pitch=pitch):
            r = it // nblk
            blk = it % nblk
            qs = qb[pl.ds(pl.multiple_of(it * B, B), B), :]
            kstart = pl.multiple_of(r * pitch + blk * B, B)
            kcat = kb[pl.ds(kstart, 2 * B), :]
            vcat = vb[pl.ds(kstart, 2 * B), :]
            no_prev = jnp.where((n == 0) & (blk == 0), 1, 0)
            s = _dot_nt(qs, kcat) + bias_ref[no_prev]
            m = jnp.max(s, axis=1, keepdims=True)
            p = jnp.exp(s - m).astype(BF16)
            pv = _dot(p, jnp.concatenate([vcat, ones], axis=1))
            start = r + blk * B * d
            dst = pl.ds(start, B, stride=d) if d > 1 else pl.ds(pl.multiple_of(start, B), B)
            acc_ref[bi, dst, :] = pv[:, :ATT_HD]
            l_ref[bi, dst, :] = pv[:, ATT_HD:]
            m_ref[bi, dst, :] = jnp.broadcast_to(m, (B, LANES))
            return carry

        lax.fori_loop(0, T // B, body, 0, unroll=8)

    m0, m1, m2 = m_ref[0], m_ref[1], m_ref[2]
    mm = jnp.maximum(jnp.maximum(m0, m1), m2)
    c0, c1, c2 = jnp.exp(m0 - mm), jnp.exp(m1 - mm), jnp.exp(m2 - mm)
    num = c0 * acc_ref[0] + c1 * acc_ref[1] + c2 * acc_ref[2]
    den = c0 * l_ref[0] + c1 * l_ref[1] + c2 * l_ref[2]
    o_ref[...] = (num / den).astype(o_ref.dtype)


def _attention(proj):
    s = proj.shape[0]
    T = ATT_SUPER
    qc, kc, vc = COL_AQ // ATT_HD, COL_AK // ATT_HD, COL_AV // ATT_HD
    blk = (T, ATT_HD)
    nb = len(ATT_DILATIONS)
    q_bufs = [pltpu.VMEM((T, ATT_HD), BF16) for _ in ATT_DILATIONS]
    kv_bufs = [pltpu.VMEM((T + d * ATT_SPAN, ATT_HD), BF16) for d in ATT_DILATIONS]
    return pl.pallas_call(
        _att_kernel,
        grid=(ATT_HEADS, s // T),
        in_specs=[pl.BlockSpec(blk, lambda h, n: (n, qc + h)),
                  pl.BlockSpec(blk, lambda h, n: (n, kc + h)),
                  pl.BlockSpec(blk, lambda h, n: (n, vc + h))],
        out_specs=pl.BlockSpec(blk, lambda h, n: (n, h)),
        out_shape=jax.ShapeDtypeStruct((s, ATT_WIDTH), BF16),
        scratch_shapes=q_bufs + kv_bufs + kv_bufs
        + [pltpu.VMEM((2, ATT_SPAN, 2 * ATT_SPAN), F32)]
        + [pltpu.VMEM((nb, T, LANES), F32)] * 3,
        compiler_params=_params("arbitrary", "arbitrary"),
        name="dilated_attn",
    )(proj, proj, proj)


def _layer_norm(z, g, b):
    mu = jnp.mean(z, axis=-1, keepdims=True)
    zc = z - mu
    var = jnp.mean(zc * zc, axis=-1, keepdims=True)
    return zc * lax.rsqrt(var + LN_EPS) * g + b


def _outproj_kernel(og_ref, oa_ref, wg_ref, wa_ref, x_ref, gate_ref, lg_ref, lb_ref, o_ref):
    y = _dot(og_ref[...], wg_ref[...]) + _dot(oa_ref[...], wa_ref[...])
    z = DEEPNORM_ALPHA * x_ref[...] + (1.0 + gate_ref[...]) * y
    o_ref[...] = _layer_norm(z, lg_ref[...], lb_ref[...])


def _outproj(og, oa, wo_g, wo_a, x, gate, ln_g, ln_b):
    s, d = x.shape
    tm = min(s, 512)
    vec = pl.BlockSpec((1, d), lambda i: (0, 0))
    return pl.pallas_call(
        _outproj_kernel,
        grid=(s // tm,),
        in_specs=[pl.BlockSpec((tm, GLA_WIDTH), lambda i: (i, 0)),
                  pl.BlockSpec((tm, ATT_WIDTH), lambda i: (i, 0)),
                  pl.BlockSpec((GLA_WIDTH, d), lambda i: (0, 0)),
                  pl.BlockSpec((ATT_WIDTH, d), lambda i: (0, 0)),
                  pl.BlockSpec((tm, d), lambda i: (i, 0)),
                  vec, vec, vec],
        out_specs=pl.BlockSpec((tm, d), lambda i: (i, 0)),
        out_shape=jax.ShapeDtypeStruct((s, d), F32),
        compiler_params=_params("arbitrary"),
        name="out_proj_ln",
    )(og, oa, wo_g, wo_a, x, gate, ln_g, ln_b)


def _ffn_kernel(x_ref, sc_ref, sh_ref, gate_ref, wa_ref, wg_ref, cwa_ref, cwg_ref,
                cba_ref, cbg_ref, wd_ref, lg_ref, lb_ref, o_ref,
                u_ref, ha_ref, hg_ref, ta_ref, tg_ref, *, tm, nf):
    i = pl.program_id(0)
    j = pl.program_id(1)
    H = 8
    acc_ref = o_ref

    @pl.when(j == 0)
    def _():
        u_ref[...] = (x_ref[...] * (1.0 + sc_ref[...]) + sh_ref[...]).astype(BF16)
        acc_ref[...] = jnp.zeros_like(acc_ref)

    @pl.when(i == 0)
    def _():
        ta_ref[j] = jnp.zeros(ta_ref.shape[1:], F32)
        tg_ref[j] = jnp.zeros(tg_ref.shape[1:], F32)

    ha_ref[0:H, :] = ta_ref[j]
    hg_ref[0:H, :] = tg_ref[j]

    R = min(tm, FFN_ROWS)

    def up(r):
        u = u_ref[r:r + R, :]
        ha = _dot(u, wa_ref[...])
        hg = _dot(u, wg_ref[...])
        ha_ref[H + r:H + r + R, :] = ha
        hg_ref[H + r:H + r + R, :] = hg
        return ha, hg

    def conv(h, cw_ref, cb_ref, h_ref, r):
        cw = cw_ref[...]
        return (cb_ref[...] + cw[0:1, :] * h_ref[H + r - 2:H + r - 2 + R, :]
                + cw[1:2, :] * h_ref[H + r - 1:H + r - 1 + R, :] + cw[2:3, :] * h)

    nxt = up(0)
    for r in range(0, tm, R):
        ha, hg = nxt
        if r + R < tm:
            nxt = up(r + R)
        ya = conv(ha, cwa_ref, cba_ref, ha_ref, r)
        yg = conv(hg, cwg_ref, cbg_ref, hg_ref, r)
        act = (yg / (1.0 + jnp.exp(-yg)) * ya).astype(BF16)
        acc_ref[r:r + R, :] += _dot(act, wd_ref[...])

    ta_ref[j] = ha_ref[tm:tm + H, :]
    tg_ref[j] = hg_ref[tm:tm + H, :]

    @pl.when(j == nf - 1)
    def _():
        z = DEEPNORM_ALPHA * x_ref[...] + (1.0 + gate_ref[...]) * acc_ref[...]
        o_ref[...] = _layer_norm(z, lg_ref[...], lb_ref[...])


def _ffn(x, sc, sh, gate, w_up, conv_w, conv_b, w_down, ln_g, ln_b):
    s, d = x.shape
    f = w_down.shape[0]
    tm = min(s, 1024)
    fc = 512
    nf = f // fc
    vec = pl.BlockSpec((1, d), lambda i, j: (0, 0))
    return pl.pallas_call(
        functools.partial(_ffn_kernel, tm=tm, nf=nf),
        grid=(s // tm, nf),
        in_specs=[pl.BlockSpec((tm, d), lambda i, j: (i, 0), pipeline_mode=pl.Buffered(1)),
                  vec, vec, vec,
                  pl.BlockSpec((d, fc), lambda i, j: (0, j)),
                  pl.BlockSpec((d, fc), lambda i, j: (0, j + nf)),
                  pl.BlockSpec((3, fc), lambda i, j: (0, j)),
                  pl.BlockSpec((3, fc), lambda i, j: (0, j + nf)),
                  pl.BlockSpec((1, fc), lambda i, j: (0, j)),
                  pl.BlockSpec((1, fc), lambda i, j: (0, j + nf)),
                  pl.BlockSpec((fc, d), lambda i, j: (j, 0)),
                  vec, vec],
        out_specs=pl.BlockSpec((tm, d), lambda i, j: (i, 0)),
        out_shape=jax.ShapeDtypeStruct((s, d), F32),
        scratch_shapes=[pltpu.VMEM((tm, d), BF16),
                        pltpu.VMEM((tm + 8, fc), F32),
                        pltpu.VMEM((tm + 8, fc), F32),
                        pltpu.VMEM((nf, 8, fc), F32),
                        pltpu.VMEM((nf, 8, fc), F32)],
        compiler_params=_params("arbitrary", "arbitrary"),
        name="conv_ffn_ln",
    )(x, sc, sh, gate, w_up, w_up, conv_w, conv_w, conv_b, conv_b, w_down, ln_g, ln_b)


def kernel(x, c, positions, w_ada, b_ada, w_in, w_gla_gate, b_gla_gate, gla_norm_g, w_out,
           ln1_g, ln1_b, w_up, conv_w, conv_b, w_down, ln2_g, ln2_b):
    batch, s, d = x.shape
    half = ATT_HD // 2
    inv_freq = ROPE_THETA ** (-jnp.arange(half, dtype=F32) / half)
    invf = jnp.concatenate([inv_freq, inv_freq]).reshape(1, ATT_HD)
    sign = jnp.concatenate([-jnp.ones((half,), F32), jnp.ones((half,), F32)]).reshape(1, ATT_HD)
    lr0 = COL_GR + GLA_WIDTH
    outs = []
    for bi in range(batch):
        xb = x[bi]
        for layer in range(w_in.shape[0]):
            mod = _ada(c[bi].reshape(d, 1), w_ada[layer], b_ada[layer].reshape(1, -1))
            sh1, sc1, g1, sh2, sc2, g2 = [mod[:, k * d:(k + 1) * d] for k in range(6)]
            wl = w_in[layer]
            w_main = jnp.concatenate([wl[:, :lr0].astype(BF16), wl[:, lr0 + GLA_RANK:].astype(BF16)],
                                     axis=1)
            w_lr = jnp.pad(wl[:, lr0:lr0 + GLA_RANK], ((0, 0), (0, LANES - GLA_RANK))).astype(BF16)
            wg_pad = jnp.pad(w_gla_gate[layer], ((0, LANES - GLA_RANK), (0, 0)))
            cos, sin = _rope_tables(positions[bi].reshape(s, 1), invf, sign)
            proj, glr = _inproj(xb, sc1, sh1, w_main, w_lr, cos, sin)
            og = _gla(proj, glr, wg_pad, b_gla_gate[layer].reshape(1, -1),
                      gla_norm_g[layer].reshape(1, -1))
            oa = _attention(proj)
            wo = w_out[layer].astype(BF16)
            xb = _outproj(og, oa, wo[:GLA_WIDTH], wo[GLA_WIDTH:], xb, g1,
                          ln1_g[layer].reshape(1, -1), ln1_b[layer].reshape(1, -1))
            xb = _ffn(xb, sc2, sh2, g2, w_up[layer].astype(BF16), conv_w[layer],
                      conv_b[layer].reshape(1, -1), w_down[layer].astype(BF16),
                      ln2_g[layer].reshape(1, -1), ln2_b[layer].reshape(1, -1))
        outs.append(xb)
    return jnp.stack(outs, axis=0)
```

```python
import functools

import jax
import jax.numpy as jnp
from jax import lax
from jax.experimental import pallas as pl
from jax.experimental.pallas import tpu as pltpu

F32 = jnp.float32
BF16 = jnp.bfloat16

D_MODEL = 2048
GLA_HEADS = 4
GLA_DK = 128
GLA_DV = 256
GLA_WIDTH = GLA_HEADS * GLA_DV
GLA_QK = GLA_HEADS * GLA_DK
GLA_RANK = 16
GLA_TAU = 16.0
ATT_HEADS = 8
ATT_HD = 128
ATT_WIDTH = ATT_HEADS * ATT_HD
ATT_SPAN = 128
ATT_DILATIONS = (1, 4, 16)
D_FF = 5632
ROPE_THETA = 10000.0
DEEPNORM_ALPHA = 2.0 ** 0.25
LN_EPS = 1e-5
RMS_EPS = 1e-6

LANES = 128
GLA_CHUNK = 128
GLA_LEVELS = (64, 32, 16, 8)
GLA_DIAG = 8
ATT_SUPER = ATT_SPAN * max(ATT_DILATIONS)
FFN_ROWS = 512
NEG_BIG = -1e30
VMEM_LIMIT = 56 * 1024 * 1024

COL_AQ, COL_AK = 0, 1024
COL_GQ, COL_GK, COL_GV, COL_GR = 2048, 2560, 3072, 4096
COL_AV = 5120
PROJ_COLS = 6144
PROJ_TN = 1024
PROJ_SUB = 512


def _dot(a, b):
    return jnp.dot(a, b, preferred_element_type=F32)


def _dot_nt(a, b):
    return lax.dot_general(a, b, (((1,), (1,)), ((), ())), preferred_element_type=F32)


def _dot_tn(a, b):
    return lax.dot_general(a, b, (((0,), (0,)), ((), ())), preferred_element_type=F32)


def _split_bf16(a):
    hi = a.astype(BF16)
    lo = (a - hi.astype(F32)).astype(BF16)
    return hi, lo


def _params(*sem, flags=None):
    return pltpu.CompilerParams(dimension_semantics=sem, vmem_limit_bytes=VMEM_LIMIT, flags=flags)


def _ada_kernel(c_ref, w_ref, b_ref, o_ref):
    cc = c_ref[...]
    s = cc / (1.0 + jnp.exp(-cc))
    o_ref[...] = jnp.sum(s * w_ref[...], axis=0, keepdims=True) + b_ref[...]


def _ada(c_col, w_ada, b_ada):
    d, n = w_ada.shape
    tn = 1024
    return pl.pallas_call(
        _ada_kernel,
        grid=(n // tn,),
        in_specs=[pl.BlockSpec((d, 1), lambda j: (0, 0)),
                  pl.BlockSpec((d, tn), lambda j: (0, j)),
                  pl.BlockSpec((1, tn), lambda j: (0, j))],
        out_specs=pl.BlockSpec((1, tn), lambda j: (0, j)),
        out_shape=jax.ShapeDtypeStruct((1, n), F32),
        compiler_params=_params("arbitrary"),
        name="ada_mod",
    )(c_col, w_ada, b_ada)


def _rope_kernel(pos_ref, invf_ref, sign_ref, cos_ref, sin_ref):
    ang = pos_ref[...].astype(F32) * invf_ref[...]
    cos_ref[...] = jnp.cos(ang)
    sin_ref[...] = jnp.sin(ang) * sign_ref[...]


def _rope_tables(pos_col, invf, sign):
    s = pos_col.shape[0]
    tm = min(s, 1024)
    return pl.pallas_call(
        _rope_kernel,
        grid=(s // tm,),
        in_specs=[pl.BlockSpec((tm, 1), lambda i: (i, 0)),
                  pl.BlockSpec((1, LANES), lambda i: (0, 0)),
                  pl.BlockSpec((1, LANES), lambda i: (0, 0))],
        out_specs=[pl.BlockSpec((tm, LANES), lambda i: (i, 0)),
                   pl.BlockSpec((tm, LANES), lambda i: (i, 0))],
        out_shape=[jax.ShapeDtypeStruct((s, LANES), F32)] * 2,
        compiler_params=_params("arbitrary"),
        name="rope_tables",
    )(pos_col, invf, sign)


def _inproj_kernel(x_ref, sc_ref, sh_ref, w_ref, wlr_ref, cos_ref, sin_ref,
                   proj_ref, glr_ref, u_ref, *, tn):
    j = pl.program_id(1)

    @pl.when(j == 0)
    def _():
        u = (x_ref[...] * (1.0 + sc_ref[...]) + sh_ref[...]).astype(BF16)
        u_ref[...] = u
        glr_ref[...] = _dot(u, wlr_ref[...])

    col = j * tn
    q_scale = ATT_HD ** -0.5

    def sub_dot(n):
        return _dot(u_ref[...], w_ref[:, n:n + PROJ_SUB])

    @pl.when(col < COL_GQ)
    def _():
        cos = cos_ref[...]
        sin = sin_ref[...]
        scale = jnp.where(col < COL_AK, q_scale, 1.0).astype(F32)
        for n in range(0, tn, PROJ_SUB):
            acc = sub_dot(n) * scale
            for h in range(n, n + PROJ_SUB, ATT_HD):
                t = acc[:, h - n:h - n + ATT_HD]
                proj_ref[:, h:h + ATT_HD] = t * cos + pltpu.roll(t, ATT_HD // 2, axis=1) * sin

    @pl.when(col >= COL_GQ)
    def _():
        for n in range(0, tn, PROJ_SUB):
            acc = sub_dot(n)
            if n < GLA_QK:
                acc = acc * jnp.where(col == COL_GQ, q_scale, 1.0).astype(F32)
            proj_ref[:, n:n + PROJ_SUB] = acc


def _inproj(x, sc, sh, w_main, w_lr, cos, sin):
    s, d = x.shape
    tm = min(s, 1024)
    tn = PROJ_TN
    return pl.pallas_call(
        functools.partial(_inproj_kernel, tn=tn),
        grid=(s // tm, PROJ_COLS // tn),
        in_specs=[pl.BlockSpec((tm, d), lambda i, j: (i, 0)),
                  pl.BlockSpec((1, d), lambda i, j: (0, 0)),
                  pl.BlockSpec((1, d), lambda i, j: (0, 0)),
                  pl.BlockSpec((d, tn), lambda i, j: (0, j)),
                  pl.BlockSpec((d, LANES), lambda i, j: (0, 0)),
                  pl.BlockSpec((tm, LANES), lambda i, j: (i, 0)),
                  pl.BlockSpec((tm, LANES), lambda i, j: (i, 0))],
        out_specs=[pl.BlockSpec((tm, tn), lambda i, j: (i, j)),
                   pl.BlockSpec((tm, LANES), lambda i, j: (i, 0))],
        out_shape=[jax.ShapeDtypeStruct((s, PROJ_COLS), F32),
                   jax.ShapeDtypeStruct((s, LANES), F32)],
        scratch_shapes=[pltpu.VMEM((tm, d), BF16)],
        compiler_params=_params("arbitrary", "arbitrary"),
        name="in_proj",
    )(x, sc, sh, w_main, w_lr, cos, sin)


def _gla_kernel(q_ref, k_ref, v_ref, gr_ref, glr_ref, wg_ref, bg_ref, gn_ref,
                o_ref, state_ref):
    c = pl.program_id(0)
    C = GLA_CHUNK

    @pl.when(c == 0)
    def _():
        state_ref[...] = jnp.zeros_like(state_ref)

    row = lax.broadcasted_iota(jnp.int32, (C, C), 0)
    col = lax.broadcasted_iota(jnp.int32, (C, C), 1)

    g_hi, g_lo = _split_bf16(glr_ref[...])
    w_hi, w_lo = _split_bf16(wg_ref[...])
    z = _dot(g_hi, w_hi) + _dot(g_hi, w_lo) + _dot(g_lo, w_hi) + bg_ref[...]
    log_a = (jnp.minimum(z, 0.0) - jnp.log1p(jnp.exp(-jnp.abs(z)))) * (1.0 / GLA_TAU)

    tri = (col <= row).astype(BF16)
    a_hi, a_lo = _split_bf16(log_a)
    bcum = _dot(tri, a_hi) + _dot(tri, a_lo)

    nlev = len(GLA_LEVELS)
    r4 = lax.broadcasted_iota(jnp.int32, (nlev * C, C), 0)
    c4 = lax.broadcasted_iota(jnp.int32, (nlev * C, C), 1)
    sel = jnp.zeros((nlev * C, C), F32)
    for li, m in enumerate(GLA_LEVELS):
        t = r4 - li * C
        hit = (t >= 0) & (t < C) & (c4 == (t // (2 * m)) * (2 * m) + m - 1)
        sel = jnp.where(hit, 1.0, sel)
    ref_all = _dot(sel.astype(BF16), bcum.astype(BF16))

    g3 = lax.broadcasted_iota(jnp.int32, (C // GLA_DIAG, GLA_DIAG, C), 0)
    t3 = lax.broadcasted_iota(jnp.int32, (C // GLA_DIAG, GLA_DIAG, C), 1)
    s3 = lax.broadcasted_iota(jnp.int32, (C // GLA_DIAG, GLA_DIAG, C), 2)

    for h in range(GLA_HEADS):
        ks = slice(h * GLA_DK, (h + 1) * GLA_DK)
        vs = slice(h * GLA_DV, (h + 1) * GLA_DV)
        q = q_ref[:, ks]
        k = k_ref[:, ks]
        vb = v_ref[:, vs].astype(BF16)
        b = bcum[:, ks]
        b_last = b[C - 1:C, :]
        state = state_ref[h]

        o = _dot((q * jnp.exp(b)).astype(BF16), state.astype(BF16))

        scores = jnp.zeros((C, C), F32)
        for li, m in enumerate(GLA_LEVELS):
            ref = ref_all[li * C:(li + 1) * C, ks]
            upper = (row & m) != 0
            qm = q * jnp.exp(jnp.where(upper, b - ref, NEG_BIG))
            km = k * jnp.exp(jnp.where(upper, NEG_BIG, ref - b))
            part = _dot_nt(qm.astype(BF16), km.astype(BF16))
            same = (row // (2 * m)) == (col // (2 * m))
            scores = scores + jnp.where(same, part, 0.0)

        q3 = q.reshape(C // GLA_DIAG, GLA_DIAG, GLA_DK)
        k3 = k.reshape(C // GLA_DIAG, GLA_DIAG, GLA_DK)
        b3 = b.reshape(C // GLA_DIAG, GLA_DIAG, GLA_DK)
        diag = jnp.zeros((C // GLA_DIAG, GLA_DIAG, C), F32)
        for jj in range(GLA_DIAG):
            kj = k3[:, jj:jj + 1, :]
            bj = b3[:, jj:jj + 1, :]
            term = q3 * kj * jnp.exp(jnp.minimum(b3 - bj, 0.0))
            colsum = jnp.sum(term, axis=-1, keepdims=True)
            hit = (s3 == g3 * GLA_DIAG + jj) & (t3 >= jj)
            diag = diag + jnp.where(hit, colsum, 0.0)
        scores = scores + diag.reshape(C, C)

        o = o + _dot(scores.astype(BF16), vb)

        kl = (k * jnp.exp(b_last - b)).astype(BF16)
        decay_col = jnp.transpose(jnp.broadcast_to(jnp.exp(b_last), (C, GLA_DK)))
        decay = jnp.concatenate([decay_col, decay_col], axis=1)
        state_ref[h] = state * decay + _dot_tn(kl, vb)

        rs = lax.rsqrt(jnp.mean(o * o, axis=-1, keepdims=True) + RMS_EPS)
        gate = gr_ref[:, vs]
        gate = gate / (1.0 + jnp.exp(-gate))
        o_ref[:, vs] = (o * rs * gn_ref[...] * gate).astype(o_ref.dtype)


def _gla(proj, glr, wg_pad, bg, gnorm):
    s = proj.shape[0]
    C = GLA_CHUNK
    return pl.pallas_call(
        _gla_kernel,
        grid=(s // C,),
        in_specs=[pl.BlockSpec((C, GLA_QK), lambda c: (c, COL_GQ // GLA_QK)),
                  pl.BlockSpec((C, GLA_QK), lambda c: (c, COL_GK // GLA_QK)),
                  pl.BlockSpec((C, GLA_WIDTH), lambda c: (c, COL_GV // GLA_WIDTH)),
                  pl.BlockSpec((C, GLA_WIDTH), lambda c: (c, COL_GR // GLA_WIDTH)),
                  pl.BlockSpec((C, LANES), lambda c: (c, 0)),
                  pl.BlockSpec((LANES, GLA_QK), lambda c: (0, 0)),
                  pl.BlockSpec((1, GLA_QK), lambda c: (0, 0)),
                  pl.BlockSpec((1, GLA_DV), lambda c: (0, 0))],
        out_specs=pl.BlockSpec((C, GLA_WIDTH), lambda c: (c, 0)),
        out_shape=jax.ShapeDtypeStruct((s, GLA_WIDTH), BF16),
        scratch_shapes=[pltpu.VMEM((GLA_HEADS, GLA_DK, GLA_DV), F32)],
        compiler_params=_params("arbitrary"),
        name="gla",
    )(proj, proj, proj, proj, glr, wg_pad, bg, gnorm)


def _att_kernel(q_ref, k_ref, v_ref, o_ref, q1, q4, q16, k1, k4, k16, v1, v4, v16,
                bias_ref, acc_ref, m_ref, l_ref):
    n = pl.program_id(1)
    B = ATT_SPAN
    T = ATT_SUPER
    qd, kd, vd = (q1, q4, q16), (k1, k4, k16), (v1, v4, v16)

    row = lax.broadcasted_iota(jnp.int32, (B, 2 * B), 0)
    col = lax.broadcasted_iota(jnp.int32, (B, 2 * B), 1)
    band = (col >= row) & (col <= row + B)
    bias_ref[0] = jnp.where(band, 0.0, NEG_BIG)
    bias_ref[1] = jnp.where(band & (col >= B), 0.0, NEG_BIG)

    @pl.when(n == 0)
    def _():
        for d, kb, vb in zip(ATT_DILATIONS, kd, vd):
            pitch = T // d + B
            for r in range(d):
                kb[r * pitch:r * pitch + B, :] = jnp.zeros((B, ATT_HD), BF16)
                vb[r * pitch:r * pitch + B, :] = jnp.zeros((B, ATT_HD), BF16)

    @pl.when(n > 0)
    def _():
        for d, kb, vb in zip(ATT_DILATIONS, kd, vd):
            pitch = T // d + B
            for r in range(d):
                kb[r * pitch:r * pitch + B, :] = kb[(r + 1) * pitch - B:(r + 1) * pitch, :]
                vb[r * pitch:r * pitch + B, :] = vb[(r + 1) * pitch - B:(r + 1) * pitch, :]

    for d, qb, kb, vb in zip(ATT_DILATIONS, qd, kd, vd):
        L = T // d
        pitch = L + B
        for r in range(d):
            rows = pl.ds(r, L, stride=d) if d > 1 else pl.ds(0, L)
            qb[r * L:(r + 1) * L, :] = q_ref[rows, :].astype(BF16)
            kb[r * pitch + B:(r + 1) * pitch, :] = k_ref[rows, :].astype(BF16)
            vb[r * pitch + B:(r + 1) * pitch, :] = v_ref[rows, :].astype(BF16)

    ones = jnp.ones((2 * B, ATT_HD), BF16)

    for bi, (d, qb, kb, vb) in enumerate(zip(ATT_DILATIONS, qd, kd, vd)):
        nblk = T // (B * d)
        pitch = T // d + B

        def body(it, carry, bi=bi, d=d, qb=qb, kb=kb, vb=vb, nblk=nblk, pitch=pitch):
            r = it // nblk
            blk = it % nblk
            qs = qb[pl.ds(pl.multiple_of(it * B, B), B), :]
            kstart = pl.multiple_of(r * pitch + blk * B, B)
            kcat = kb[pl.ds(kstart, 2 * B), :]
            vcat = vb[pl.ds(kstart, 2 * B), :]
            no_prev = jnp.where((n == 0) & (blk == 0), 1, 0)
            s = _dot_nt(qs, kcat) + bias_ref[no_prev]
            m = jnp.max(s, axis=1, keepdims=True)
            p = jnp.exp(s - m).astype(BF16)
            pv = _dot(p, jnp.concatenate([vcat, ones], axis=1))
            start = r + blk * B * d
            dst = pl.ds(start, B, stride=d) if d > 1 else pl.ds(pl.multiple_of(start, B), B)
            acc_ref[bi, dst, :] = pv[:, :ATT_HD]
            l_ref[bi, dst, :] = pv[:, ATT_HD:]
            m_ref[bi, dst, :] = jnp.broadcast_to(m, (B, LANES))
            return carry

        lax.fori_loop(0, T // B, body, 0, unroll=8)

    m0, m1, m2 = m_ref[0], m_ref[1], m_ref[2]
    mm = jnp.maximum(jnp.maximum(m0, m1), m2)
    c0, c1, c2 = jnp.exp(m0 - mm), jnp.exp(m1 - mm), jnp.exp(m2 - mm)
    num = c0 * acc_ref[0] + c1 * acc_ref[1] + c2 * acc_ref[2]
    den = c0 * l_ref[0] + c1 * l_ref[1] + c2 * l_ref[2]
    o_ref[...] = (num / den).astype(o_ref.dtype)


def _attention(proj):
    s = proj.shape[0]
    T = ATT_SUPER
    qc, kc, vc = COL_AQ // ATT_HD, COL_AK // ATT_HD, COL_AV // ATT_HD
    blk = (T, ATT_HD)
    nb = len(ATT_DILATIONS)
    q_bufs = [pltpu.VMEM((T, ATT_HD), BF16) for _ in ATT_DILATIONS]
    kv_bufs = [pltpu.VMEM((T + d * ATT_SPAN, ATT_HD), BF16) for d in ATT_DILATIONS]
    return pl.pallas_call(
        _att_kernel,
        grid=(ATT_HEADS, s // T),
        in_specs=[pl.BlockSpec(blk, lambda h, n: (n, qc + h)),
                  pl.BlockSpec(blk, lambda h, n: (n, kc + h)),
                  pl.BlockSpec(blk, lambda h, n: (n, vc + h))],
        out_specs=pl.BlockSpec(blk, lambda h, n: (n, h)),
        out_shape=jax.ShapeDtypeStruct((s, ATT_WIDTH), BF16),
        scratch_shapes=q_bufs + kv_bufs + kv_bufs
        + [pltpu.VMEM((2, ATT_SPAN, 2 * ATT_SPAN), F32)]
        + [pltpu.VMEM((nb, T, LANES), F32)] * 3,
        compiler_params=_params("arbitrary", "arbitrary"),
        name="dilated_attn",
    )(proj, proj, proj)


def _layer_norm(z, g, b):
    mu = jnp.mean(z, axis=-1, keepdims=True)
    zc = z - mu
    var = jnp.mean(zc * zc, axis=-1, keepdims=True)
    return zc * lax.rsqrt(var + LN_EPS) * g + b


def _outproj_kernel(og_ref, oa_ref, wg_ref, wa_ref, x_ref, gate_ref, lg_ref, lb_ref, o_ref):
    y = _dot(og_ref[...], wg_ref[...]) + _dot(oa_ref[...], wa_ref[...])
    z = DEEPNORM_ALPHA * x_ref[...] + (1.0 + gate_ref[...]) * y
    o_ref[...] = _layer_norm(z, lg_ref[...], lb_ref[...])


def _outproj(og, oa, wo_g, wo_a, x, gate, ln_g, ln_b):
    s, d = x.shape
    tm = min(s, 512)
    vec = pl.BlockSpec((1, d), lambda i: (0, 0))
    return pl.pallas_call(
        _outproj_kernel,
        grid=(s // tm,),
        in_specs=[pl.BlockSpec((tm, GLA_WIDTH), lambda i: (i, 0)),
                  pl.BlockSpec((tm, ATT_WIDTH), lambda i: (i, 0)),
                  pl.BlockSpec((GLA_WIDTH, d), lambda i: (0, 0)),
                  pl.BlockSpec((ATT_WIDTH, d), lambda i: (0, 0)),
                  pl.BlockSpec((tm, d), lambda i: (i, 0)),
                  vec, vec, vec],
        out_specs=pl.BlockSpec((tm, d), lambda i: (i, 0)),
        out_shape=jax.ShapeDtypeStruct((s, d), F32),
        compiler_params=_params("arbitrary"),
        name="out_proj_ln",
    )(og, oa, wo_g, wo_a, x, gate, ln_g, ln_b)


def _ffn_kernel(x_ref, sc_ref, sh_ref, gate_ref, wa_ref, wg_ref, cwa_ref, cwg_ref,
                cba_ref, cbg_ref, wd_ref, lg_ref, lb_ref, o_ref,
                u_ref, ha_ref, hg_ref, ta_ref, tg_ref, *, tm, nf):
    i = pl.program_id(0)
    j = pl.program_id(1)
    H = 8
    acc_ref = o_ref

    @pl.when(j == 0)
    def _():
        u_ref[...] = (x_ref[...] * (1.0 + sc_ref[...]) + sh_ref[...]).astype(BF16)
        acc_ref[...] = jnp.zeros_like(acc_ref)

    @pl.when(i == 0)
    def _():
        ta_ref[j] = jnp.zeros(ta_ref.shape[1:], F32)
        tg_ref[j] = jnp.zeros(tg_ref.shape[1:], F32)

    ha_ref[0:H, :] = ta_ref[j]
    hg_ref[0:H, :] = tg_ref[j]

    R = min(tm, FFN_ROWS)

    def up(r):
        u = u_ref[r:r + R, :]
        ha = _dot(u, wa_ref[...])
        hg = _dot(u, wg_ref[...])
        ha_ref[H + r:H + r + R, :] = ha
        hg_ref[H + r:H + r + R, :] = hg
        return ha, hg

    def conv(h, cw_ref, cb_ref, h_ref, r):
        cw = cw_ref[...]
        return (cb_ref[...] + cw[0:1, :] * h_ref[H + r - 2:H + r - 2 + R, :]
                + cw[1:2, :] * h_ref[H + r - 1:H + r - 1 + R, :] + cw[2:3, :] * h)

    nxt = up(0)
    for r in range(0, tm, R):
        ha, hg = nxt
        if r + R < tm:
            nxt = up(r + R)
        ya = conv(ha, cwa_ref, cba_ref, ha_ref, r)
        yg = conv(hg, cwg_ref, cbg_ref, hg_ref, r)
        act = (yg / (1.0 + jnp.exp(-yg)) * ya).astype(BF16)
        acc_ref[r:r + R, :] += _dot(act, wd_ref[...])

    ta_ref[j] = ha_ref[tm:tm + H, :]
    tg_ref[j] = hg_ref[tm:tm + H, :]

    @pl.when(j == nf - 1)
    def _():
        z = DEEPNORM_ALPHA * x_ref[...] + (1.0 + gate_ref[...]) * acc_ref[...]
        o_ref[...] = _layer_norm(z, lg_ref[...], lb_ref[...])


def _ffn(x, sc, sh, gate, w_up, conv_w, conv_b, w_down, ln_g, ln_b):
    s, d = x.shape
    f = w_down.shape[0]
    tm = min(s, 1024)
    fc = 512
    nf = f // fc
    vec = pl.BlockSpec((1, d), lambda i, j: (0, 0))
    return pl.pallas_call(
        functools.partial(_ffn_kernel, tm=tm, nf=nf),
        grid=(s // tm, nf),
        in_specs=[pl.BlockSpec((tm, d), lambda i, j: (i, 0), pipeline_mode=pl.Buffered(1)),
                  vec, vec, vec,
                  pl.BlockSpec((d, fc), lambda i, j: (0, j)),
                  pl.BlockSpec((d, fc), lambda i, j: (0, j + nf)),
                  pl.BlockSpec((3, fc), lambda i, j: (0, j)),
                  pl.BlockSpec((3, fc), lambda i, j: (0, j + nf)),
                  pl.BlockSpec((1, fc), lambda i, j: (0, j)),
                  pl.BlockSpec((1, fc), lambda i, j: (0, j + nf)),
                  pl.BlockSpec((fc, d), lambda i, j: (j, 0)),
                  vec, vec],
        out_specs=pl.BlockSpec((tm, d), lambda i, j: (i, 0)),
        out_shape=jax.ShapeDtypeStruct((s, d), F32),
        scratch_shapes=[pltpu.VMEM((tm, d), BF16),
                        pltpu.VMEM((tm + 8, fc), F32),
                        pltpu.VMEM((tm + 8, fc), F32),
                        pltpu.VMEM((nf, 8, fc), F32),
                        pltpu.VMEM((nf, 8, fc), F32)],
        compiler_params=_params("arbitrary", "arbitrary"),
        name="conv_ffn_ln",
    )(x, sc, sh, gate, w_up, w_up, conv_w, conv_w, conv_b, conv_b, w_down, ln_g, ln_b)


def kernel(x, c, positions, w_ada, b_ada, w_in, w_gla_gate, b_gla_gate, gla_norm_g, w_out,
           ln1_g, ln1_b, w_up, conv_w, conv_b, w_down, ln2_g, ln2_b):
    batch, s, d = x.shape
    half = ATT_HD // 2
    inv_freq = ROPE_THETA ** (-jnp.arange(half, dtype=F32) / half)
    invf = jnp.concatenate([inv_freq, inv_freq]).reshape(1, ATT_HD)
    sign = jnp.concatenate([-jnp.ones((half,), F32), jnp.ones((half,), F32)]).reshape(1, ATT_HD)
    lr0 = 2 * GLA_QK + 2 * GLA_WIDTH
    av0 = lr0 + GLA_RANK + 2 * ATT_WIDTH
    outs = []
    for bi in range(batch):
        xb = x.reshape(s, d) if batch == 1 else x[bi]
        for layer in range(w_in.shape[0]):
            mod = _ada(c[bi].reshape(d, 1), w_ada[layer], b_ada[layer].reshape(1, -1))
            sh1, sc1, g1, sh2, sc2, g2 = [mod[:, k * d:(k + 1) * d] for k in range(6)]
            wl = w_in[layer]
            w_main = jnp.concatenate([wl[:, lr0 + GLA_RANK:av0].astype(BF16), wl[:, :lr0].astype(BF16),
                                      wl[:, av0:].astype(BF16)], axis=1)
            w_lr = jnp.pad(wl[:, lr0:lr0 + GLA_RANK], ((0, 0), (0, LANES - GLA_RANK))).astype(BF16)
            wg_pad = jnp.pad(w_gla_gate[layer], ((0, LANES - GLA_RANK), (0, 0)))
            cos, sin = _rope_tables(positions[bi].reshape(s, 1), invf, sign)
            proj, glr = _inproj(xb, sc1, sh1, w_main, w_lr, cos, sin)
            og = _gla(proj, glr, wg_pad, b_gla_gate[layer].reshape(1, -1),
                      gla_norm_g[layer].reshape(1, -1))
            oa = _attention(proj)
            wo = w_out[layer].astype(BF16)
            xb = _outproj(og, oa, wo[:GLA_WIDTH], wo[GLA_WIDTH:], xb, g1,
                          ln1_g[layer].reshape(1, -1), ln1_b[layer].reshape(1, -1))
            xb = _ffn(xb, sc2, sh2, g2, w_up[layer].astype(BF16), conv_w[layer],
                      conv_b[layer].reshape(1, -1), w_down[layer].astype(BF16),
                      ln2_g[layer].reshape(1, -1), ln2_b[layer].reshape(1, -1))
        outs.append(xb)
    return outs[0].reshape(1, s, d) if batch == 1 else jnp.stack(outs, axis=0)
```

```python
import functools

import jax
import jax.numpy as jnp
from jax import lax
from jax.experimental import pallas as pl
from jax.experimental.pallas import tpu as pltpu

F32 = jnp.float32
BF16 = jnp.bfloat16

D_MODEL = 2048
GLA_HEADS = 4
GLA_DK = 128
GLA_DV = 256
GLA_WIDTH = GLA_HEADS * GLA_DV
GLA_QK = GLA_HEADS * GLA_DK
GLA_RANK = 16
GLA_TAU = 16.0
ATT_HEADS = 8
ATT_HD = 128
ATT_WIDTH = ATT_HEADS * ATT_HD
ATT_SPAN = 128
ATT_DILATIONS = (1, 4, 16)
D_FF = 5632
ROPE_THETA = 10000.0
DEEPNORM_ALPHA = 2.0 ** 0.25
LN_EPS = 1e-5
RMS_EPS = 1e-6

LANES = 128
GLA_CHUNK = 128
GLA_LEVELS = (64, 32, 16, 8)
GLA_DIAG = 8
ATT_SUPER = ATT_SPAN * max(ATT_DILATIONS)
FFN_ROWS = 512
NEG_BIG = -1e30
VMEM_LIMIT = 56 * 1024 * 1024

COL_AQ, COL_AK = 0, 1024
COL_GQ, COL_GK, COL_GV, COL_GR = 2048, 2560, 3072, 4096
COL_AV = 5120
PROJ_COLS = 6144
PROJ_TN = 1024
PROJ_SUB = 512


def _dot(a, b):
    return jnp.dot(a, b, preferred_element_type=F32)


def _dot_nt(a, b):
    return lax.dot_general(a, b, (((1,), (1,)), ((), ())), preferred_element_type=F32)


def _dot_tn(a, b):
    return lax.dot_general(a, b, (((0,), (0,)), ((), ())), preferred_element_type=F32)


def _split_bf16(a):
    hi = a.astype(BF16)
    lo = (a - hi.astype(F32)).astype(BF16)
    return hi, lo


def _params(*sem, flags=None):
    return pltpu.CompilerParams(dimension_semantics=sem, vmem_limit_bytes=VMEM_LIMIT, flags=flags)


def _ada_kernel(c_ref, w_ref, b_ref, o_ref):
    cc = c_ref[...]
    s = cc / (1.0 + jnp.exp(-cc))
    o_ref[...] = jnp.sum(s * w_ref[...], axis=0, keepdims=True) + b_ref[...]


def _ada(c_col, w_ada, b_ada):
    d, n = w_ada.shape
    tn = 1024
    return pl.pallas_call(
        _ada_kernel,
        grid=(n // tn,),
        in_specs=[pl.BlockSpec((d, 1), lambda j: (0, 0)),
                  pl.BlockSpec((d, tn), lambda j: (0, j)),
                  pl.BlockSpec((1, tn), lambda j: (0, j))],
        out_specs=pl.BlockSpec((1, tn), lambda j: (0, j)),
        out_shape=jax.ShapeDtypeStruct((1, n), F32),
        compiler_params=_params("arbitrary"),
        name="ada_mod",
    )(c_col, w_ada, b_ada)


def _rope_kernel(pos_ref, invf_ref, sign_ref, cos_ref, sin_ref):
    ang = pos_ref[...].astype(F32) * invf_ref[...]
    cos_ref[...] = jnp.cos(ang)
    sin_ref[...] = jnp.sin(ang) * sign_ref[...]


def _rope_tables(pos_col, invf, sign):
    s = pos_col.shape[0]
    tm = min(s, 1024)
    return pl.pallas_call(
        _rope_kernel,
        grid=(s // tm,),
        in_specs=[pl.BlockSpec((tm, 1), lambda i: (i, 0)),
                  pl.BlockSpec((1, LANES), lambda i: (0, 0)),
                  pl.BlockSpec((1, LANES), lambda i: (0, 0))],
        out_specs=[pl.BlockSpec((tm, LANES), lambda i: (i, 0)),
                   pl.BlockSpec((tm, LANES), lambda i: (i, 0))],
        out_shape=[jax.ShapeDtypeStruct((s, LANES), F32)] * 2,
        compiler_params=_params("arbitrary"),
        name="rope_tables",
    )(pos_col, invf, sign)


def _wprep_kernel(w_ref, main_ref, lr_ref):
    lr0 = 2 * GLA_QK + 2 * GLA_WIDTH
    qk0 = lr0 + GLA_RANK
    av0 = qk0 + 2 * ATT_WIDTH
    main_ref[:, COL_AQ:COL_GQ] = w_ref[:, qk0:av0].astype(BF16)
    main_ref[:, COL_GQ:COL_AV] = w_ref[:, 0:lr0].astype(BF16)
    main_ref[:, COL_AV:PROJ_COLS] = w_ref[:, av0:av0 + ATT_WIDTH].astype(BF16)
    lr = w_ref[:, lr0:lr0 + LANES]
    lane = lax.broadcasted_iota(jnp.int32, lr.shape, 1)
    lr_ref[...] = jnp.where(lane < GLA_RANK, lr, 0.0).astype(BF16)


def _wprep(w_in):
    d, n = w_in.shape
    tr = 256
    return pl.pallas_call(
        _wprep_kernel,
        grid=(d // tr,),
        in_specs=[pl.BlockSpec((tr, n), lambda i: (i, 0))],
        out_specs=[pl.BlockSpec((tr, PROJ_COLS), lambda i: (i, 0)),
                   pl.BlockSpec((tr, LANES), lambda i: (i, 0))],
        out_shape=[jax.ShapeDtypeStruct((d, PROJ_COLS), BF16),
                   jax.ShapeDtypeStruct((d, LANES), BF16)],
        compiler_params=_params("arbitrary"),
        name="w_in_prep",
    )(w_in)


def _inproj_kernel(x_ref, sc_ref, sh_ref, w_ref, wlr_ref, cos_ref, sin_ref,
                   proj_ref, glr_ref, u_ref, *, tn):
    j = pl.program_id(1)

    @pl.when(j == 0)
    def _():
        u = (x_ref[...] * (1.0 + sc_ref[...]) + sh_ref[...]).astype(BF16)
        u_ref[...] = u
        glr_ref[...] = _dot(u, wlr_ref[...])

    col = j * tn
    q_scale = ATT_HD ** -0.5

    def sub_dot(n):
        return _dot(u_ref[...], w_ref[:, n:n + PROJ_SUB])

    @pl.when(col < COL_GQ)
    def _():
        cos = cos_ref[...]
        sin = sin_ref[...]
        scale = jnp.where(col < COL_AK, q_scale, 1.0).astype(F32)
        for n in range(0, tn, PROJ_SUB):
            acc = sub_dot(n) * scale
            for h in range(n, n + PROJ_SUB, ATT_HD):
                t = acc[:, h - n:h - n + ATT_HD]
                proj_ref[:, h:h + ATT_HD] = t * cos + pltpu.roll(t, ATT_HD // 2, axis=1) * sin

    @pl.when(col >= COL_GQ)
    def _():
        for n in range(0, tn, PROJ_SUB):
            acc = sub_dot(n)
            if n < GLA_QK:
                acc = acc * jnp.where(col == COL_GQ, q_scale, 1.0).astype(F32)
            proj_ref[:, n:n + PROJ_SUB] = acc


def _inproj(x, sc, sh, w_main, w_lr, cos, sin):
    s, d = x.shape
    tm = min(s, 1024)
    tn = PROJ_TN
    return pl.pallas_call(
        functools.partial(_inproj_kernel, tn=tn),
        grid=(s // tm, PROJ_COLS // tn),
        in_specs=[pl.BlockSpec((tm, d), lambda i, j: (i, 0)),
                  pl.BlockSpec((1, d), lambda i, j: (0, 0)),
                  pl.BlockSpec((1, d), lambda i, j: (0, 0)),
                  pl.BlockSpec((d, tn), lambda i, j: (0, j)),
                  pl.BlockSpec((d, LANES), lambda i, j: (0, 0)),
                  pl.BlockSpec((tm, LANES), lambda i, j: (i, 0)),
                  pl.BlockSpec((tm, LANES), lambda i, j: (i, 0))],
        out_specs=[pl.BlockSpec((tm, tn), lambda i, j: (i, j)),
                   pl.BlockSpec((tm, LANES), lambda i, j: (i, 0))],
        out_shape=[jax.ShapeDtypeStruct((s, PROJ_COLS), F32),
                   jax.ShapeDtypeStruct((s, LANES), F32)],
        scratch_shapes=[pltpu.VMEM((tm, d), BF16)],
        compiler_params=_params("arbitrary", "arbitrary"),
        name="in_proj",
    )(x, sc, sh, w_main, w_lr, cos, sin)


def _gla_kernel(q_ref, k_ref, v_ref, gr_ref, glr_ref, wg_ref, bg_ref, gn_ref,
                o_ref, state_ref):
    c = pl.program_id(0)
    C = GLA_CHUNK

    @pl.when(c == 0)
    def _():
        state_ref[...] = jnp.zeros_like(state_ref)

    row = lax.broadcasted_iota(jnp.int32, (C, C), 0)
    col = lax.broadcasted_iota(jnp.int32, (C, C), 1)

    g_hi, g_lo = _split_bf16(glr_ref[...])
    w_hi, w_lo = _split_bf16(wg_ref[...])
    z = _dot(g_hi, w_hi) + _dot(g_hi, w_lo) + _dot(g_lo, w_hi) + bg_ref[...]
    log_a = (jnp.minimum(z, 0.0) - jnp.log1p(jnp.exp(-jnp.abs(z)))) * (1.0 / GLA_TAU)

    tri = (col <= row).astype(BF16)
    a_hi, a_lo = _split_bf16(log_a)
    bcum = _dot(tri, a_hi) + _dot(tri, a_lo)

    nlev = len(GLA_LEVELS)
    r4 = lax.broadcasted_iota(jnp.int32, (nlev * C, C), 0)
    c4 = lax.broadcasted_iota(jnp.int32, (nlev * C, C), 1)
    sel = jnp.zeros((nlev * C, C), F32)
    for li, m in enumerate(GLA_LEVELS):
        t = r4 - li * C
        hit = (t >= 0) & (t < C) & (c4 == (t // (2 * m)) * (2 * m) + m - 1)
        sel = jnp.where(hit, 1.0, sel)
    ref_all = _dot(sel.astype(BF16), bcum.astype(BF16))

    g3 = lax.broadcasted_iota(jnp.int32, (C // GLA_DIAG, GLA_DIAG, C), 0)
    t3 = lax.broadcasted_iota(jnp.int32, (C // GLA_DIAG, GLA_DIAG, C), 1)
    s3 = lax.broadcasted_iota(jnp.int32, (C // GLA_DIAG, GLA_DIAG, C), 2)

    for h in range(GLA_HEADS):
        ks = slice(h * GLA_DK, (h + 1) * GLA_DK)
        vs = slice(h * GLA_DV, (h + 1) * GLA_DV)
        q = q_ref[:, ks]
        k = k_ref[:, ks]
        vb = v_ref[:, vs].astype(BF16)
        b = bcum[:, ks]
        b_last = b[C - 1:C, :]
        state = state_ref[h]

        o = _dot((q * jnp.exp(b)).astype(BF16), state.astype(BF16))

        scores = jnp.zeros((C, C), F32)
        for li, m in enumerate(GLA_LEVELS):
            ref = ref_all[li * C:(li + 1) * C, ks]
            upper = (row & m) != 0
            qm = q * jnp.exp(jnp.where(upper, b - ref, NEG_BIG))
            km = k * jnp.exp(jnp.where(upper, NEG_BIG, ref - b))
            part = _dot_nt(qm.astype(BF16), km.astype(BF16))
            same = (row // (2 * m)) == (col // (2 * m))
            scores = scores + jnp.where(same, part, 0.0)

        q3 = q.reshape(C // GLA_DIAG, GLA_DIAG, GLA_DK)
        k3 = k.reshape(C // GLA_DIAG, GLA_DIAG, GLA_DK)
        b3 = b.reshape(C // GLA_DIAG, GLA_DIAG, GLA_DK)
        diag = jnp.zeros((C // GLA_DIAG, GLA_DIAG, C), F32)
        for jj in range(GLA_DIAG):
            kj = k3[:, jj:jj + 1, :]
            bj = b3[:, jj:jj + 1, :]
            term = q3 * kj * jnp.exp(jnp.minimum(b3 - bj, 0.0))
            colsum = jnp.sum(term, axis=-1, keepdims=True)
            hit = (s3 == g3 * GLA_DIAG + jj) & (t3 >= jj)
            diag = diag + jnp.where(hit, colsum, 0.0)
        scores = scores + diag.reshape(C, C)

        o = o + _dot(scores.astype(BF16), vb)

        kl = (k * jnp.exp(b_last - b)).astype(BF16)
        decay_col = jnp.transpose(jnp.broadcast_to(jnp.exp(b_last), (C, GLA_DK)))
        decay = jnp.concatenate([decay_col, decay_col], axis=1)
        state_ref[h] = state * decay + _dot_tn(kl, vb)

        rs = lax.rsqrt(jnp.mean(o * o, axis=-1, keepdims=True) + RMS_EPS)
        gate = gr_ref[:, vs]
        gate = gate / (1.0 + jnp.exp(-gate))
        o_ref[:, vs] = (o * rs * gn_ref[...] * gate).astype(o_ref.dtype)


def _gla(proj, glr, wg_pad, bg, gnorm):
    s = proj.shape[0]
    C = GLA_CHUNK
    return pl.pallas_call(
        _gla_kernel,
        grid=(s // C,),
        in_specs=[pl.BlockSpec((C, GLA_QK), lambda c: (c, COL_GQ // GLA_QK)),
                  pl.BlockSpec((C, GLA_QK), lambda c: (c, COL_GK // GLA_QK)),
                  pl.BlockSpec((C, GLA_WIDTH), lambda c: (c, COL_GV // GLA_WIDTH)),
                  pl.BlockSpec((C, GLA_WIDTH), lambda c: (c, COL_GR // GLA_WIDTH)),
                  pl.BlockSpec((C, LANES), lambda c: (c, 0)),
                  pl.BlockSpec((LANES, GLA_QK), lambda c: (0, 0)),
                  pl.BlockSpec((1, GLA_QK), lambda c: (0, 0)),
                  pl.BlockSpec((1, GLA_DV), lambda c: (0, 0))],
        out_specs=pl.BlockSpec((C, GLA_WIDTH), lambda c: (c, 0)),
        out_shape=jax.ShapeDtypeStruct((s, GLA_WIDTH), BF16),
        scratch_shapes=[pltpu.VMEM((GLA_HEADS, GLA_DK, GLA_DV), F32)],
        compiler_params=_params("arbitrary"),
        name="gla",
    )(proj, proj, proj, proj, glr, wg_pad, bg, gnorm)


def _att_kernel(q_ref, k_ref, v_ref, o_ref, q1, q4, q16, k1, k4, k16, v1, v4, v16,
                qs_ref, ks_ref, vs_ref, bias_ref, acc_ref, lse_ref):
    n = pl.program_id(1)
    B = ATT_SPAN
    T = ATT_SUPER
    qd, kd, vd = (q1, q4, q16), (k1, k4, k16), (v1, v4, v16)

    row = lax.broadcasted_iota(jnp.int32, (B, 2 * B), 0)
    col = lax.broadcasted_iota(jnp.int32, (B, 2 * B), 1)
    band = (col >= row) & (col <= row + B)
    bias_ref[0] = jnp.where(band, 0.0, NEG_BIG)
    bias_ref[1] = jnp.where(band & (col >= B), 0.0, NEG_BIG)

    @pl.when(n == 0)
    def _():
        for d, kb, vb in zip(ATT_DILATIONS, kd, vd):
            pitch = T // d + B
            for r in range(d):
                kb[r * pitch:r * pitch + B, :] = jnp.zeros((B, ATT_HD), BF16)
                vb[r * pitch:r * pitch + B, :] = jnp.zeros((B, ATT_HD), BF16)

    @pl.when(n > 0)
    def _():
        for d, kb, vb in zip(ATT_DILATIONS, kd, vd):
            pitch = T // d + B
            for r in range(d):
                kb[r * pitch:r * pitch + B, :] = kb[(r + 1) * pitch - B:(r + 1) * pitch, :]
                vb[r * pitch:r * pitch + B, :] = vb[(r + 1) * pitch - B:(r + 1) * pitch, :]

    assert ATT_DILATIONS == (1, 4, 16)
    for src, stage, bufs, front in ((q_ref, qs_ref, qd, 0), (k_ref, ks_ref, kd, B), (v_ref, vs_ref, vd, B)):
        b1, b4, b16 = bufs
        b1[front:front + T, :] = src[...].astype(BF16)
        L4, L16 = T // 4, T // 16
        for r in range(4):
            val = src[pl.ds(r, L4, stride=4), :]
            stage[r * L4:(r + 1) * L4, :] = val
            lo = r * (L4 + front) + front
            b4[lo:lo + L4, :] = val.astype(BF16)
        for r in range(16):
            val = stage[pl.ds((r % 4) * L4 + r // 4, L16, stride=4), :]
            lo = r * (L16 + front) + front
            b16[lo:lo + L16, :] = val.astype(BF16)

    ones = jnp.ones((2 * B, ATT_HD), BF16)

    for bi, (d, qb, kb, vb) in enumerate(zip(ATT_DILATIONS, qd, kd, vd)):
        nblk = T // (B * d)
        pitch = T // d + B

        def body(it, carry, bi=bi, d=d, qb=qb, kb=kb, vb=vb, nblk=nblk, pitch=pitch):
            r = it // nblk
            blk = it % nblk
            qs = qb[pl.ds(pl.multiple_of(it * B, B), B), :]
            kstart = pl.multiple_of(r * pitch + blk * B, B)
            kcat = kb[pl.ds(kstart, 2 * B), :]
            vcat = vb[pl.ds(kstart, 2 * B), :]
            no_prev = jnp.where((n == 0) & (blk == 0), 1, 0)
            s = _dot_nt(qs, kcat) + bias_ref[no_prev]
            m = jnp.max(s, axis=1, keepdims=True)
            p = jnp.exp(s - m).astype(BF16)
            pv = _dot(p, jnp.concatenate([vcat, ones], axis=1))
            start = r + blk * B * d
            dst = pl.ds(start, B, stride=d) if d > 1 else pl.ds(pl.multiple_of(start, B), B)
            l = pv[:, ATT_HD:]
            acc_ref[bi, dst, :] = pv[:, :ATT_HD] / l
            lse_ref[bi, dst, :] = m + jnp.log(l)
            return carry

        lax.fori_loop(0, T // B, body, 0, unroll=16)

    s0, s1, s2 = lse_ref[0], lse_ref[1], lse_ref[2]
    mm = jnp.maximum(jnp.maximum(s0, s1), s2)
    c0, c1, c2 = jnp.exp(s0 - mm), jnp.exp(s1 - mm), jnp.exp(s2 - mm)
    num = c0 * acc_ref[0] + c1 * acc_ref[1] + c2 * acc_ref[2]
    o_ref[...] = (num / (c0 + c1 + c2)).astype(o_ref.dtype)


def _attention(proj):
    s = proj.shape[0]
    T = ATT_SUPER
    qc, kc, vc = COL_AQ // ATT_HD, COL_AK // ATT_HD, COL_AV // ATT_HD
    blk = (T, ATT_HD)
    nb = len(ATT_DILATIONS)
    q_bufs = [pltpu.VMEM((T, ATT_HD), BF16) for _ in ATT_DILATIONS]
    kv_bufs = [pltpu.VMEM((T + d * ATT_SPAN, ATT_HD), BF16) for d in ATT_DILATIONS]
    return pl.pallas_call(
        _att_kernel,
        grid=(ATT_HEADS, s // T),
        in_specs=[pl.BlockSpec(blk, lambda h, n: (n, qc + h)),
                  pl.BlockSpec(blk, lambda h, n: (n, kc + h)),
                  pl.BlockSpec(blk, lambda h, n: (n, vc + h))],
        out_specs=pl.BlockSpec(blk, lambda h, n: (n, h)),
        out_shape=jax.ShapeDtypeStruct((s, ATT_WIDTH), BF16),
        scratch_shapes=q_bufs + kv_bufs + kv_bufs
        + [pltpu.VMEM((T, ATT_HD), F32)] * 3
        + [pltpu.VMEM((2, ATT_SPAN, 2 * ATT_SPAN), F32)]
        + [pltpu.VMEM((nb, T, LANES), F32)] * 2,
        compiler_params=_params("arbitrary", "arbitrary"),
        name="dilated_attn",
    )(proj, proj, proj)


def _layer_norm(z, g, b):
    mu = jnp.mean(z, axis=-1, keepdims=True)
    zc = z - mu
    var = jnp.mean(zc * zc, axis=-1, keepdims=True)
    return zc * lax.rsqrt(var + LN_EPS) * g + b


def _outproj_kernel(og_ref, oa_ref, wg_ref, wa_ref, x_ref, gate_ref, lg_ref, lb_ref, o_ref):
    y = _dot(og_ref[...], wg_ref[...]) + _dot(oa_ref[...], wa_ref[...])
    z = DEEPNORM_ALPHA * x_ref[...] + (1.0 + gate_ref[...]) * y
    o_ref[...] = _layer_norm(z, lg_ref[...], lb_ref[...])


def _outproj(og, oa, wo_g, wo_a, x, gate, ln_g, ln_b):
    s, d = x.shape
    tm = min(s, 512)
    vec = pl.BlockSpec((1, d), lambda i: (0, 0))
    return pl.pallas_call(
        _outproj_kernel,
        grid=(s // tm,),
        in_specs=[pl.BlockSpec((tm, GLA_WIDTH), lambda i: (i, 0)),
                  pl.BlockSpec((tm, ATT_WIDTH), lambda i: (i, 0)),
                  pl.BlockSpec((GLA_WIDTH, d), lambda i: (0, 0)),
                  pl.BlockSpec((ATT_WIDTH, d), lambda i: (0, 0)),
                  pl.BlockSpec((tm, d), lambda i: (i, 0)),
                  vec, vec, vec],
        out_specs=pl.BlockSpec((tm, d), lambda i: (i, 0)),
        out_shape=jax.ShapeDtypeStruct((s, d), F32),
        compiler_params=_params("arbitrary"),
        name="out_proj_ln",
    )(og, oa, wo_g, wo_a, x, gate, ln_g, ln_b)


def _ffn_kernel(x_ref, sc_ref, sh_ref, gate_ref, wa_ref, wg_ref, cwa_ref, cwg_ref,
                cba_ref, cbg_ref, wd_ref, lg_ref, lb_ref, o_ref,
                u_ref, ha_ref, hg_ref, ta_ref, tg_ref, *, tm, nf):
    i = pl.program_id(0)
    j = pl.program_id(1)
    H = 8
    acc_ref = o_ref

    @pl.when(j == 0)
    def _():
        u_ref[...] = (x_ref[...] * (1.0 + sc_ref[...]) + sh_ref[...]).astype(BF16)
        acc_ref[...] = jnp.zeros_like(acc_ref)

    @pl.when(i == 0)
    def _():
        ta_ref[j] = jnp.zeros(ta_ref.shape[1:], F32)
        tg_ref[j] = jnp.zeros(tg_ref.shape[1:], F32)

    ha_ref[0:H, :] = ta_ref[j]
    hg_ref[0:H, :] = tg_ref[j]

    R = min(tm, FFN_ROWS)

    def up(r):
        u = u_ref[r:r + R, :]
        ha = _dot(u, wa_ref[...])
        hg = _dot(u, wg_ref[...])
        ha_ref[H + r:H + r + R, :] = ha
        hg_ref[H + r:H + r + R, :] = hg
        return ha, hg

    def conv(h, cw_ref, cb_ref, h_ref, r):
        cw = cw_ref[...]
        return (cb_ref[...] + cw[0:1, :] * h_ref[H + r - 2:H + r - 2 + R, :]
                + cw[1:2, :] * h_ref[H + r - 1:H + r - 1 + R, :] + cw[2:3, :] * h)

    nxt = up(0)
    for r in range(0, tm, R):
        ha, hg = nxt
        if r + R < tm:
            nxt = up(r + R)
        ya = conv(ha, cwa_ref, cba_ref, ha_ref, r)
        yg = conv(hg, cwg_ref, cbg_ref, hg_ref, r)
        act = (yg / (1.0 + jnp.exp(-yg)) * ya).astype(BF16)
        acc_ref[r:r + R, :] += _dot(act, wd_ref[...])

    ta_ref[j] = ha_ref[tm:tm + H, :]
    tg_ref[j] = hg_ref[tm:tm + H, :]

    @pl.when(j == nf - 1)
    def _():
        z = DEEPNORM_ALPHA * x_ref[...] + (1.0 + gate_ref[...]) * acc_ref[...]
        o_ref[...] = _layer_norm(z, lg_ref[...], lb_ref[...])


def _ffn(x, sc, sh, gate, w_up, conv_w, conv_b, w_down, ln_g, ln_b):
    s, d = x.shape
    f = w_down.shape[0]
    tm = min(s, 1024)
    fc = 512
    nf = f // fc
    vec = pl.BlockSpec((1, d), lambda i, j: (0, 0))
    return pl.pallas_call(
        functools.partial(_ffn_kernel, tm=tm, nf=nf),
        grid=(s // tm, nf),
        in_specs=[pl.BlockSpec((tm, d), lambda i, j: (i, 0), pipeline_mode=pl.Buffered(1)),
                  vec, vec, vec,
                  pl.BlockSpec((d, fc), lambda i, j: (0, j)),
                  pl.BlockSpec((d, fc), lambda i, j: (0, j + nf)),
                  pl.BlockSpec((3, fc), lambda i, j: (0, j)),
                  pl.BlockSpec((3, fc), lambda i, j: (0, j + nf)),
                  pl.BlockSpec((1, fc), lambda i, j: (0, j)),
                  pl.BlockSpec((1, fc), lambda i, j: (0, j + nf)),
                  pl.BlockSpec((fc, d), lambda i, j: (j, 0)),
                  vec, vec],
        out_specs=pl.BlockSpec((tm, d), lambda i, j: (i, 0)),
        out_shape=jax.ShapeDtypeStruct((s, d), F32),
        scratch_shapes=[pltpu.VMEM((tm, d), BF16),
                        pltpu.VMEM((tm + 8, fc), F32),
                        pltpu.VMEM((tm + 8, fc), F32),
                        pltpu.VMEM((nf, 8, fc), F32),
                        pltpu.VMEM((nf, 8, fc), F32)],
        compiler_params=_params("arbitrary", "arbitrary"),
        name="conv_ffn_ln",
    )(x, sc, sh, gate, w_up, w_up, conv_w, conv_w, conv_b, conv_b, w_down, ln_g, ln_b)


def kernel(x, c, positions, w_ada, b_ada, w_in, w_gla_gate, b_gla_gate, gla_norm_g, w_out,
           ln1_g, ln1_b, w_up, conv_w, conv_b, w_down, ln2_g, ln2_b):
    batch, s, d = x.shape
    half = ATT_HD // 2
    inv_freq = ROPE_THETA ** (-jnp.arange(half, dtype=F32) / half)
    invf = jnp.concatenate([inv_freq, inv_freq]).reshape(1, ATT_HD)
    sign = jnp.concatenate([-jnp.ones((half,), F32), jnp.ones((half,), F32)]).reshape(1, ATT_HD)
    outs = []
    for bi in range(batch):
        xb = x.reshape(s, d) if batch == 1 else x[bi]
        for layer in range(w_in.shape[0]):
            mod = _ada(c[bi].reshape(d, 1), w_ada[layer], b_ada[layer].reshape(1, -1))
            sh1, sc1, g1, sh2, sc2, g2 = [mod[:, k * d:(k + 1) * d] for k in range(6)]
            w_main, w_lr = _wprep(w_in[layer])
            wg_pad = jnp.pad(w_gla_gate[layer], ((0, LANES - GLA_RANK), (0, 0)))
            cos, sin = _rope_tables(positions[bi].reshape(s, 1), invf, sign)
            proj, glr = _inproj(xb, sc1, sh1, w_main, w_lr, cos, sin)
            og = _gla(proj, glr, wg_pad, b_gla_gate[layer].reshape(1, -1),
                      gla_norm_g[layer].reshape(1, -1))
            oa = _attention(proj)
            wo = w_out[layer].astype(BF16)
            xb = _outproj(og, oa, wo[:GLA_WIDTH], wo[GLA_WIDTH:], xb, g1,
                          ln1_g[layer].reshape(1, -1), ln1_b[layer].reshape(1, -1))
            xb = _ffn(xb, sc2, sh2, g2, w_up[layer].astype(BF16), conv_w[layer],
                      conv_b[layer].reshape(1, -1), w_down[layer].astype(BF16),
                      ln2_g[layer].reshape(1, -1), ln2_b[layer].reshape(1, -1))
        outs.append(xb)
    return outs[0].reshape(1, s, d) if batch == 1 else jnp.stack(outs, axis=0)
```

```python
import functools

import jax
import jax.numpy as jnp
from jax import lax
from jax.experimental import pallas as pl
from jax.experimental.pallas import tpu as pltpu

F32 = jnp.float32
BF16 = jnp.bfloat16

D_MODEL = 2048
GLA_HEADS = 4
GLA_DK = 128
GLA_DV = 256
GLA_WIDTH = GLA_HEADS * GLA_DV
GLA_QK = GLA_HEADS * GLA_DK
GLA_RANK = 16
GLA_TAU = 16.0
ATT_HEADS = 8
ATT_HD = 128
ATT_WIDTH = ATT_HEADS * ATT_HD
ATT_SPAN = 128
ATT_DILATIONS = (1, 4, 16)
D_FF = 5632
ROPE_THETA = 10000.0
DEEPNORM_ALPHA = 2.0 ** 0.25
LN_EPS = 1e-5
RMS_EPS = 1e-6

LANES = 128
GLA_CHUNK = 128
GLA_LEVELS = (64, 32, 16, 8)
GLA_DIAG = 8
ATT_SUPER = ATT_SPAN * max(ATT_DILATIONS)
FFN_ROWS = 512
NEG_BIG = -1e30
VMEM_LIMIT = 56 * 1024 * 1024

COL_AQ, COL_AK = 0, 1024
COL_GQ, COL_GK, COL_GV, COL_GR = 2048, 2560, 3072, 4096
COL_AV = 5120
PROJ_COLS = 6144
PROJ_TN = 1024
PROJ_SUB = 512


def _dot(a, b):
    return jnp.dot(a, b, preferred_element_type=F32)


def _dot_nt(a, b):
    return lax.dot_general(a, b, (((1,), (1,)), ((), ())), preferred_element_type=F32)


def _dot_tn(a, b):
    return lax.dot_general(a, b, (((0,), (0,)), ((), ())), preferred_element_type=F32)


def _split_bf16(a):
    hi = a.astype(BF16)
    lo = (a - hi.astype(F32)).astype(BF16)
    return hi, lo


def _params(*sem, flags=None):
    return pltpu.CompilerParams(dimension_semantics=sem, vmem_limit_bytes=VMEM_LIMIT, flags=flags)


def _ada_kernel(c_ref, w_ref, b_ref, o_ref):
    cc = c_ref[...]
    s = cc / (1.0 + jnp.exp(-cc))
    o_ref[...] = jnp.sum(s * w_ref[...], axis=0, keepdims=True) + b_ref[...]


def _ada(c_col, w_ada, b_ada):
    d, n = w_ada.shape
    tn = 1024
    return pl.pallas_call(
        _ada_kernel,
        grid=(n // tn,),
        in_specs=[pl.BlockSpec((d, 1), lambda j: (0, 0)),
                  pl.BlockSpec((d, tn), lambda j: (0, j)),
                  pl.BlockSpec((1, tn), lambda j: (0, j))],
        out_specs=pl.BlockSpec((1, tn), lambda j: (0, j)),
        out_shape=jax.ShapeDtypeStruct((1, n), F32),
        compiler_params=_params("arbitrary"),
        name="ada_mod",
    )(c_col, w_ada, b_ada)


def _rope_kernel(pos_ref, invf_ref, sign_ref, cos_ref, sin_ref):
    ang = pos_ref[...].astype(F32) * invf_ref[...]
    cos_ref[...] = jnp.cos(ang)
    sin_ref[...] = jnp.sin(ang) * sign_ref[...]


def _rope_tables(pos_col, invf, sign):
    s = pos_col.shape[0]
    tm = min(s, 1024)
    return pl.pallas_call(
        _rope_kernel,
        grid=(s // tm,),
        in_specs=[pl.BlockSpec((tm, 1), lambda i: (i, 0)),
                  pl.BlockSpec((1, LANES), lambda i: (0, 0)),
                  pl.BlockSpec((1, LANES), lambda i: (0, 0))],
        out_specs=[pl.BlockSpec((tm, LANES), lambda i: (i, 0)),
                   pl.BlockSpec((tm, LANES), lambda i: (i, 0))],
        out_shape=[jax.ShapeDtypeStruct((s, LANES), F32)] * 2,
        compiler_params=_params("arbitrary"),
        name="rope_tables",
    )(pos_col, invf, sign)


SRC_LR = 2 * GLA_QK + 2 * GLA_WIDTH
SRC_AQ = SRC_LR + GLA_RANK
SRC_AV = SRC_AQ + 2 * ATT_WIDTH


def _wprep_kernel(w_ref, lrsrc_ref, main_ref, lr_ref):
    main_ref[...] = w_ref[...].astype(BF16)

    @pl.when(pl.program_id(0) == 0)
    def _():
        lr_ref[0:GLA_RANK, :] = lrsrc_ref[...].astype(BF16)
        lr_ref[GLA_RANK:, :] = jnp.zeros((LANES - GLA_RANK, lr_ref.shape[1]), BF16)


def _wprep(w_t):
    n, d = w_t.shape
    tr = 256

    def src_row(i):
        r = i * (tr // 8)
        tile = jnp.where(r < COL_GQ // 8, SRC_AQ // 8 + r,
                         jnp.where(r < COL_AV // 8, r - COL_GQ // 8, (SRC_AV - COL_AV) // 8 + r))
        return tile * 8

    return pl.pallas_call(
        _wprep_kernel,
        grid=(PROJ_COLS // tr,),
        in_specs=[pl.BlockSpec((pl.Element(tr), pl.Element(d)), lambda i: (src_row(i), 0)),
                  pl.BlockSpec((pl.Element(GLA_RANK), pl.Element(d)), lambda i: (SRC_LR, 0))],
        out_specs=[pl.BlockSpec((tr, d), lambda i: (i, 0)),
                   pl.BlockSpec((LANES, d), lambda i: (0, 0))],
        out_shape=[jax.ShapeDtypeStruct((PROJ_COLS, d), BF16),
                   jax.ShapeDtypeStruct((LANES, d), BF16)],
        compiler_params=_params("arbitrary"),
        name="w_in_prep",
    )(w_t, w_t)


def _inproj_kernel(x_ref, sc_ref, sh_ref, w_ref, wlr_ref, cos_ref, sin_ref,
                   proj_ref, glr_ref, u_ref, *, tn):
    j = pl.program_id(1)

    @pl.when(j == 0)
    def _():
        u = (x_ref[...] * (1.0 + sc_ref[...]) + sh_ref[...]).astype(BF16)
        u_ref[...] = u
        glr_ref[...] = _dot_nt(u, wlr_ref[...])

    col = j * tn
    q_scale = ATT_HD ** -0.5

    def sub_dot(n):
        return _dot_nt(u_ref[...], w_ref[n:n + PROJ_SUB, :])

    @pl.when(col < COL_GQ)
    def _():
        cos = cos_ref[...]
        sin = sin_ref[...]
        scale = jnp.where(col < COL_AK, q_scale, 1.0).astype(F32)
        for n in range(0, tn, PROJ_SUB):
            acc = sub_dot(n) * scale
            for h in range(n, n + PROJ_SUB, ATT_HD):
                t = acc[:, h - n:h - n + ATT_HD]
                proj_ref[:, h:h + ATT_HD] = t * cos + pltpu.roll(t, ATT_HD // 2, axis=1) * sin

    @pl.when(col >= COL_GQ)
    def _():
        for n in range(0, tn, PROJ_SUB):
            acc = sub_dot(n)
            if n < GLA_QK:
                acc = acc * jnp.where(col == COL_GQ, q_scale, 1.0).astype(F32)
            proj_ref[:, n:n + PROJ_SUB] = acc


def _inproj(x, sc, sh, w_main, w_lr, cos, sin):
    s, d = x.shape
    tm = min(s, 1024)
    tn = PROJ_TN
    return pl.pallas_call(
        functools.partial(_inproj_kernel, tn=tn),
        grid=(s // tm, PROJ_COLS // tn),
        in_specs=[pl.BlockSpec((tm, d), lambda i, j: (i, 0)),
                  pl.BlockSpec((1, d), lambda i, j: (0, 0)),
                  pl.BlockSpec((1, d), lambda i, j: (0, 0)),
                  pl.BlockSpec((tn, d), lambda i, j: (j, 0)),
                  pl.BlockSpec((LANES, d), lambda i, j: (0, 0)),
                  pl.BlockSpec((tm, LANES), lambda i, j: (i, 0)),
                  pl.BlockSpec((tm, LANES), lambda i, j: (i, 0))],
        out_specs=[pl.BlockSpec((tm, tn), lambda i, j: (i, j)),
                   pl.BlockSpec((tm, LANES), lambda i, j: (i, 0))],
        out_shape=[jax.ShapeDtypeStruct((s, PROJ_COLS), F32),
                   jax.ShapeDtypeStruct((s, LANES), F32)],
        scratch_shapes=[pltpu.VMEM((tm, d), BF16)],
        compiler_params=_params("arbitrary", "arbitrary"),
        name="in_proj",
    )(x, sc, sh, w_main, w_lr, cos, sin)


def _gla_kernel(q_ref, k_ref, v_ref, gr_ref, glr_ref, wg_ref, bg_ref, gn_ref,
                o_ref, state_ref):
    c = pl.program_id(0)
    C = GLA_CHUNK

    @pl.when(c == 0)
    def _():
        state_ref[...] = jnp.zeros_like(state_ref)

    row = lax.broadcasted_iota(jnp.int32, (C, C), 0)
    col = lax.broadcasted_iota(jnp.int32, (C, C), 1)

    g_hi, g_lo = _split_bf16(glr_ref[...])
    w_hi, w_lo = _split_bf16(wg_ref[...])
    z = _dot(g_hi, w_hi) + _dot(g_hi, w_lo) + _dot(g_lo, w_hi) + bg_ref[...]
    log_a = (jnp.minimum(z, 0.0) - jnp.log1p(jnp.exp(-jnp.abs(z)))) * (1.0 / GLA_TAU)

    tri = (col <= row).astype(BF16)
    a_hi, a_lo = _split_bf16(log_a)
    bcum = _dot(tri, a_hi) + _dot(tri, a_lo)

    nlev = len(GLA_LEVELS)
    r4 = lax.broadcasted_iota(jnp.int32, (nlev * C, C), 0)
    c4 = lax.broadcasted_iota(jnp.int32, (nlev * C, C), 1)
    sel = jnp.zeros((nlev * C, C), F32)
    for li, m in enumerate(GLA_LEVELS):
        t = r4 - li * C
        hit = (t >= 0) & (t < C) & (c4 == (t // (2 * m)) * (2 * m) + m - 1)
        sel = jnp.where(hit, 1.0, sel)
    ref_all = _dot(sel.astype(BF16), bcum.astype(BF16))

    g3 = lax.broadcasted_iota(jnp.int32, (C // GLA_DIAG, GLA_DIAG, C), 0)
    t3 = lax.broadcasted_iota(jnp.int32, (C // GLA_DIAG, GLA_DIAG, C), 1)
    s3 = lax.broadcasted_iota(jnp.int32, (C // GLA_DIAG, GLA_DIAG, C), 2)

    for h in range(GLA_HEADS):
        ks = slice(h * GLA_DK, (h + 1) * GLA_DK)
        vs = slice(h * GLA_DV, (h + 1) * GLA_DV)
        q = q_ref[:, ks]
        k = k_ref[:, ks]
        vb = v_ref[:, vs].astype(BF16)
        b = bcum[:, ks]
        b_last = b[C - 1:C, :]
        state = state_ref[h]

        o = _dot((q * jnp.exp(b)).astype(BF16), state.astype(BF16))

        scores = jnp.zeros((C, C), F32)
        for li, m in enumerate(GLA_LEVELS):
            ref = ref_all[li * C:(li + 1) * C, ks]
            upper = (row & m) != 0
            qm = q * jnp.exp(jnp.where(upper, b - ref, NEG_BIG))
            km = k * jnp.exp(jnp.where(upper, NEG_BIG, ref - b))
            part = _dot_nt(qm.astype(BF16), km.astype(BF16))
            same = (row // (2 * m)) == (col // (2 * m))
            scores = scores + jnp.where(same, part, 0.0)

        q3 = q.reshape(C // GLA_DIAG, GLA_DIAG, GLA_DK)
        k3 = k.reshape(C // GLA_DIAG, GLA_DIAG, GLA_DK)
        b3 = b.reshape(C // GLA_DIAG, GLA_DIAG, GLA_DK)
        diag = jnp.zeros((C // GLA_DIAG, GLA_DIAG, C), F32)
        for jj in range(GLA_DIAG):
            kj = k3[:, jj:jj + 1, :]
            bj = b3[:, jj:jj + 1, :]
            term = q3 * kj * jnp.exp(jnp.minimum(b3 - bj, 0.0))
            colsum = jnp.sum(term, axis=-1, keepdims=True)
            hit = (s3 == g3 * GLA_DIAG + jj) & (t3 >= jj)
            diag = diag + jnp.where(hit, colsum, 0.0)
        scores = scores + diag.reshape(C, C)

        o = o + _dot(scores.astype(BF16), vb)

        kl = (k * jnp.exp(b_last - b)).astype(BF16)
        decay_col = jnp.transpose(jnp.broadcast_to(jnp.exp(b_last), (C, GLA_DK)))
        decay = jnp.concatenate([decay_col, decay_col], axis=1)
        state_ref[h] = state * decay + _dot_tn(kl, vb)

        rs = lax.rsqrt(jnp.mean(o * o, axis=-1, keepdims=True) + RMS_EPS)
        gate = gr_ref[:, vs]
        gate = gate / (1.0 + jnp.exp(-gate))
        o_ref[:, vs] = (o * rs * gn_ref[...] * gate).astype(o_ref.dtype)


def _gla(proj, glr, wg_pad, bg, gnorm):
    s = proj.shape[0]
    C = GLA_CHUNK
    return pl.pallas_call(
        _gla_kernel,
        grid=(s // C,),
        in_specs=[pl.BlockSpec((C, GLA_QK), lambda c: (c, COL_GQ // GLA_QK)),
                  pl.BlockSpec((C, GLA_QK), lambda c: (c, COL_GK // GLA_QK)),
                  pl.BlockSpec((C, GLA_WIDTH), lambda c: (c, COL_GV // GLA_WIDTH)),
                  pl.BlockSpec((C, GLA_WIDTH), lambda c: (c, COL_GR // GLA_WIDTH)),
                  pl.BlockSpec((C, LANES), lambda c: (c, 0)),
                  pl.BlockSpec((LANES, GLA_QK), lambda c: (0, 0)),
                  pl.BlockSpec((1, GLA_QK), lambda c: (0, 0)),
                  pl.BlockSpec((1, GLA_DV), lambda c: (0, 0))],
        out_specs=pl.BlockSpec((C, GLA_WIDTH), lambda c: (c, 0)),
        out_shape=jax.ShapeDtypeStruct((s, GLA_WIDTH), BF16),
        scratch_shapes=[pltpu.VMEM((GLA_HEADS, GLA_DK, GLA_DV), F32)],
        compiler_params=_params("arbitrary"),
        name="gla",
    )(proj, proj, proj, proj, glr, wg_pad, bg, gnorm)


def _att_kernel(q_ref, k_ref, v_ref, o_ref, q1, q4, q16, k1, k4, k16, v1, v4, v16,
                qs_ref, ks_ref, vs_ref, bias_ref, acc_ref, lse_ref):
    n = pl.program_id(1)
    B = ATT_SPAN
    T = ATT_SUPER
    qd, kd, vd = (q1, q4, q16), (k1, k4, k16), (v1, v4, v16)

    row = lax.broadcasted_iota(jnp.int32, (B, 2 * B), 0)
    col = lax.broadcasted_iota(jnp.int32, (B, 2 * B), 1)
    band = (col >= row) & (col <= row + B)
    bias_ref[0] = jnp.where(band, 0.0, NEG_BIG)
    bias_ref[1] = jnp.where(band & (col >= B), 0.0, NEG_BIG)

    @pl.when(n == 0)
    def _():
        for d, kb, vb in zip(ATT_DILATIONS, kd, vd):
            pitch = T // d + B
            for r in range(d):
                kb[r * pitch:r * pitch + B, :] = jnp.zeros((B, ATT_HD), BF16)
                vb[r * pitch:r * pitch + B, :] = jnp.zeros((B, ATT_HD), BF16)

    @pl.when(n > 0)
    def _():
        for d, kb, vb in zip(ATT_DILATIONS, kd, vd):
            pitch = T // d + B
            for r in range(d):
                kb[r * pitch:r * pitch + B, :] = kb[(r + 1) * pitch - B:(r + 1) * pitch, :]
                vb[r * pitch:r * pitch + B, :] = vb[(r + 1) * pitch - B:(r + 1) * pitch, :]

    assert ATT_DILATIONS == (1, 4, 16)
    for src, stage, bufs, front in ((q_ref, qs_ref, qd, 0), (k_ref, ks_ref, kd, B), (v_ref, vs_ref, vd, B)):
        b1, b4, b16 = bufs
        b1[front:front + T, :] = src[...].astype(BF16)
        L4, L16 = T // 4, T // 16
        for r in range(4):
            val = src[pl.ds(r, L4, stride=4), :]
            stage[r * L4:(r + 1) * L4, :] = val
            lo = r * (L4 + front) + front
            b4[lo:lo + L4, :] = val.astype(BF16)
        for r in range(16):
            val = stage[pl.ds((r % 4) * L4 + r // 4, L16, stride=4), :]
            lo = r * (L16 + front) + front
            b16[lo:lo + L16, :] = val.astype(BF16)

    ones = jnp.ones((2 * B, ATT_HD), BF16)

    for bi, (d, qb, kb, vb) in enumerate(zip(ATT_DILATIONS, qd, kd, vd)):
        nblk = T // (B * d)
        pitch = T // d + B

        def body(it, carry, bi=bi, d=d, qb=qb, kb=kb, vb=vb, nblk=nblk, pitch=pitch):
            r = it // nblk
            blk = it % nblk
            qs = qb[pl.ds(pl.multiple_of(it * B, B), B), :]
            kstart = pl.multiple_of(r * pitch + blk * B, B)
            kcat = kb[pl.ds(kstart, 2 * B), :]
            vcat = vb[pl.ds(kstart, 2 * B), :]
            no_prev = jnp.where((n == 0) & (blk == 0), 1, 0)
            s = _dot_nt(qs, kcat) + bias_ref[no_prev]
            m = jnp.max(s, axis=1, keepdims=True)
            p = jnp.exp(s - m).astype(BF16)
            pv = _dot(p, jnp.concatenate([vcat, ones], axis=1))
            start = r + blk * B * d
            dst = pl.ds(start, B, stride=d) if d > 1 else pl.ds(pl.multiple_of(start, B), B)
            l = pv[:, ATT_HD:]
            acc_ref[bi, dst, :] = pv[:, :ATT_HD] / l
            lse_ref[bi, dst, :] = m + jnp.log(l)
            return carry

        lax.fori_loop(0, T // B, body, 0, unroll=16)

    s0, s1, s2 = lse_ref[0], lse_ref[1], lse_ref[2]
    mm = jnp.maximum(jnp.maximum(s0, s1), s2)
    c0, c1, c2 = jnp.exp(s0 - mm), jnp.exp(s1 - mm), jnp.exp(s2 - mm)
    num = c0 * acc_ref[0] + c1 * acc_ref[1] + c2 * acc_ref[2]
    o_ref[...] = (num / (c0 + c1 + c2)).astype(o_ref.dtype)


def _attention(proj):
    s = proj.shape[0]
    T = ATT_SUPER
    qc, kc, vc = COL_AQ // ATT_HD, COL_AK // ATT_HD, COL_AV // ATT_HD
    blk = (T, ATT_HD)
    nb = len(ATT_DILATIONS)
    q_bufs = [pltpu.VMEM((T, ATT_HD), BF16) for _ in ATT_DILATIONS]
    kv_bufs = [pltpu.VMEM((T + d * ATT_SPAN, ATT_HD), BF16) for d in ATT_DILATIONS]
    return pl.pallas_call(
        _att_kernel,
        grid=(ATT_HEADS, s // T),
        in_specs=[pl.BlockSpec(blk, lambda h, n: (n, qc + h)),
                  pl.BlockSpec(blk, lambda h, n: (n, kc + h)),
                  pl.BlockSpec(blk, lambda h, n: (n, vc + h))],
        out_specs=pl.BlockSpec(blk, lambda h, n: (n, h)),
        out_shape=jax.ShapeDtypeStruct((s, ATT_WIDTH), BF16),
        scratch_shapes=q_bufs + kv_bufs + kv_bufs
        + [pltpu.VMEM((T, ATT_HD), F32)] * 3
        + [pltpu.VMEM((2, ATT_SPAN, 2 * ATT_SPAN), F32)]
        + [pltpu.VMEM((nb, T, LANES), F32)] * 2,
        compiler_params=_params("arbitrary", "arbitrary"),
        name="dilated_attn",
    )(proj, proj, proj)


def _layer_norm(z, g, b):
    mu = jnp.mean(z, axis=-1, keepdims=True)
    zc = z - mu
    var = jnp.mean(zc * zc, axis=-1, keepdims=True)
    return zc * lax.rsqrt(var + LN_EPS) * g + b


def _outproj_kernel(og_ref, oa_ref, wg_ref, wa_ref, x_ref, gate_ref, lg_ref, lb_ref, o_ref):
    y = _dot(og_ref[...], wg_ref[...]) + _dot(oa_ref[...], wa_ref[...])
    z = DEEPNORM_ALPHA * x_ref[...] + (1.0 + gate_ref[...]) * y
    o_ref[...] = _layer_norm(z, lg_ref[...], lb_ref[...])


def _outproj(og, oa, wo_g, wo_a, x, gate, ln_g, ln_b):
    s, d = x.shape
    tm = min(s, 512)
    vec = pl.BlockSpec((1, d), lambda i: (0, 0))
    return pl.pallas_call(
        _outproj_kernel,
        grid=(s // tm,),
        in_specs=[pl.BlockSpec((tm, GLA_WIDTH), lambda i: (i, 0)),
                  pl.BlockSpec((tm, ATT_WIDTH), lambda i: (i, 0)),
                  pl.BlockSpec((GLA_WIDTH, d), lambda i: (0, 0)),
                  pl.BlockSpec((ATT_WIDTH, d), lambda i: (0, 0)),
                  pl.BlockSpec((tm, d), lambda i: (i, 0)),
                  vec, vec, vec],
        out_specs=pl.BlockSpec((tm, d), lambda i: (i, 0)),
        out_shape=jax.ShapeDtypeStruct((s, d), F32),
        compiler_params=_params("arbitrary"),
        name="out_proj_ln",
    )(og, oa, wo_g, wo_a, x, gate, ln_g, ln_b)


def _ffn_kernel(x_ref, sc_ref, sh_ref, gate_ref, wa_ref, wg_ref, cwa_ref, cwg_ref,
                cba_ref, cbg_ref, wd_ref, lg_ref, lb_ref, o_ref,
                u_ref, ha_ref, hg_ref, ta_ref, tg_ref, *, tm, nf):
    i = pl.program_id(0)
    j = pl.program_id(1)
    H = 8
    acc_ref = o_ref

    @pl.when(j == 0)
    def _():
        u_ref[...] = (x_ref[...] * (1.0 + sc_ref[...]) + sh_ref[...]).astype(BF16)
        acc_ref[...] = jnp.zeros_like(acc_ref)

    @pl.when(i == 0)
    def _():
        ta_ref[j] = jnp.zeros(ta_ref.shape[1:], F32)
        tg_ref[j] = jnp.zeros(tg_ref.shape[1:], F32)

    ha_ref[0:H, :] = ta_ref[j]
    hg_ref[0:H, :] = tg_ref[j]

    R = min(tm, FFN_ROWS)

    def up(r):
        u = u_ref[r:r + R, :]
        ha = _dot(u, wa_ref[...])
        hg = _dot(u, wg_ref[...])
        ha_ref[H + r:H + r + R, :] = ha
        hg_ref[H + r:H + r + R, :] = hg
        return ha, hg

    def conv(h, cw_ref, cb_ref, h_ref, r):
        cw = cw_ref[...]
        return (cb_ref[...] + cw[0:1, :] * h_ref[H + r - 2:H + r - 2 + R, :]
                + cw[1:2, :] * h_ref[H + r - 1:H + r - 1 + R, :] + cw[2:3, :] * h)

    nxt = up(0)
    for r in range(0, tm, R):
        ha, hg = nxt
        if r + R < tm:
            nxt = up(r + R)
        ya = conv(ha, cwa_ref, cba_ref, ha_ref, r)
        yg = conv(hg, cwg_ref, cbg_ref, hg_ref, r)
        act = (yg / (1.0 + jnp.exp(-yg)) * ya).astype(BF16)
        acc_ref[r:r + R, :] += _dot(act, wd_ref[...])

    ta_ref[j] = ha_ref[tm:tm + H, :]
    tg_ref[j] = hg_ref[tm:tm + H, :]

    @pl.when(j == nf - 1)
    def _():
        z = DEEPNORM_ALPHA * x_ref[...] + (1.0 + gate_ref[...]) * acc_ref[...]
        o_ref[...] = _layer_norm(z, lg_ref[...], lb_ref[...])


def _ffn(x, sc, sh, gate, w_up, conv_w, conv_b, w_down, ln_g, ln_b):
    s, d = x.shape
    f = w_down.shape[0]
    tm = min(s, 1024)
    fc = 512
    nf = f // fc
    vec = pl.BlockSpec((1, d), lambda i, j: (0, 0))
    return pl.pallas_call(
        functools.partial(_ffn_kernel, tm=tm, nf=nf),
        grid=(s // tm, nf),
        in_specs=[pl.BlockSpec((tm, d), lambda i, j: (i, 0), pipeline_mode=pl.Buffered(1)),
                  vec, vec, vec,
                  pl.BlockSpec((d, fc), lambda i, j: (0, j)),
                  pl.BlockSpec((d, fc), lambda i, j: (0, j + nf)),
                  pl.BlockSpec((3, fc), lambda i, j: (0, j)),
                  pl.BlockSpec((3, fc), lambda i, j: (0, j + nf)),
                  pl.BlockSpec((1, fc), lambda i, j: (0, j)),
                  pl.BlockSpec((1, fc), lambda i, j: (0, j + nf)),
                  pl.BlockSpec((fc, d), lambda i, j: (j, 0)),
                  vec, vec],
        out_specs=pl.BlockSpec((tm, d), lambda i, j: (i, 0)),
        out_shape=jax.ShapeDtypeStruct((s, d), F32),
        scratch_shapes=[pltpu.VMEM((tm, d), BF16),
                        pltpu.VMEM((tm + 8, fc), F32),
                        pltpu.VMEM((tm + 8, fc), F32),
                        pltpu.VMEM((nf, 8, fc), F32),
                        pltpu.VMEM((nf, 8, fc), F32)],
        compiler_params=_params("arbitrary", "arbitrary"),
        name="conv_ffn_ln",
    )(x, sc, sh, gate, w_up, w_up, conv_w, conv_w, conv_b, conv_b, w_down, ln_g, ln_b)


def kernel(x, c, positions, w_ada, b_ada, w_in, w_gla_gate, b_gla_gate, gla_norm_g, w_out,
           ln1_g, ln1_b, w_up, conv_w, conv_b, w_down, ln2_g, ln2_b):
    batch, s, d = x.shape
    half = ATT_HD // 2
    inv_freq = ROPE_THETA ** (-jnp.arange(half, dtype=F32) / half)
    invf = jnp.concatenate([inv_freq, inv_freq]).reshape(1, ATT_HD)
    sign = jnp.concatenate([-jnp.ones((half,), F32), jnp.ones((half,), F32)]).reshape(1, ATT_HD)
    outs = []
    for bi in range(batch):
        xb = x.reshape(s, d) if batch == 1 else x[bi]
        for layer in range(w_in.shape[0]):
            mod = _ada(c[bi].reshape(d, 1), w_ada[layer], b_ada[layer].reshape(1, -1))
            sh1, sc1, g1, sh2, sc2, g2 = [mod[:, k * d:(k + 1) * d] for k in range(6)]
            w_main, w_lr = _wprep(jnp.transpose(w_in[layer]))
            wg_pad = jnp.pad(w_gla_gate[layer], ((0, LANES - GLA_RANK), (0, 0)))
            cos, sin = _rope_tables(positions[bi].reshape(s, 1), invf, sign)
            proj, glr = _inproj(xb, sc1, sh1, w_main, w_lr, cos, sin)
            og = _gla(proj, glr, wg_pad, b_gla_gate[layer].reshape(1, -1),
                      gla_norm_g[layer].reshape(1, -1))
            oa = _attention(proj)
            wo = w_out[layer].astype(BF16)
            xb = _outproj(og, oa, wo[:GLA_WIDTH], wo[GLA_WIDTH:], xb, g1,
                          ln1_g[layer].reshape(1, -1), ln1_b[layer].reshape(1, -1))
            xb = _ffn(xb, sc2, sh2, g2, w_up[layer].astype(BF16), conv_w[layer],
                      conv_b[layer].reshape(1, -1), w_down[layer].astype(BF16),
                      ln2_g[layer].reshape(1, -1), ln2_b[layer].reshape(1, -1))
        outs.append(xb)
    return outs[0].reshape(1, s, d) if batch == 1 else jnp.stack(outs, axis=0)
```

```python
import functools

import jax
import jax.numpy as jnp
from jax import lax
from jax.experimental import pallas as pl
from jax.experimental.pallas import tpu as pltpu

F32 = jnp.float32
BF16 = jnp.bfloat16

D_MODEL = 2048
GLA_HEADS = 4
GLA_DK = 128
GLA_DV = 256
GLA_WIDTH = GLA_HEADS * GLA_DV
GLA_QK = GLA_HEADS * GLA_DK
GLA_RANK = 16
GLA_TAU = 16.0
ATT_HEADS = 8
ATT_HD = 128
ATT_WIDTH = ATT_HEADS * ATT_HD
ATT_SPAN = 128
ATT_DILATIONS = (1, 4, 16)
D_FF = 5632
ROPE_THETA = 10000.0
DEEPNORM_ALPHA = 2.0 ** 0.25
LN_EPS = 1e-5
RMS_EPS = 1e-6

LANES = 128
GLA_CHUNK = 128
GLA_LEVELS = (64, 32, 16, 8)
GLA_DIAG = 8
ATT_SUPER = ATT_SPAN * max(ATT_DILATIONS)
FFN_ROWS = 512
NEG_BIG = -1e30
VMEM_LIMIT = 56 * 1024 * 1024

COL_AQ, COL_AK = 0, 1024
COL_GQ, COL_GK, COL_GV, COL_GR = 2048, 2560, 3072, 4096
COL_AV = 5120
PROJ_COLS = 6144
PROJ_TN = 1024
PROJ_SUB = 512


def _dot(a, b):
    return jnp.dot(a, b, preferred_element_type=F32)


def _dot_nt(a, b):
    return lax.dot_general(a, b, (((1,), (1,)), ((), ())), preferred_element_type=F32)


def _dot_tn(a, b):
    return lax.dot_general(a, b, (((0,), (0,)), ((), ())), preferred_element_type=F32)


def _split_bf16(a):
    hi = a.astype(BF16)
    lo = (a - hi.astype(F32)).astype(BF16)
    return hi, lo


def _params(*sem, flags=None):
    return pltpu.CompilerParams(dimension_semantics=sem, vmem_limit_bytes=VMEM_LIMIT, flags=flags)


def _ada_kernel(c_ref, w_ref, b_ref, o_ref):
    cc = c_ref[...]
    s = cc / (1.0 + jnp.exp(-cc))
    o_ref[...] = jnp.sum(s * w_ref[...], axis=0, keepdims=True) + b_ref[...]


def _ada(c_col, w_ada, b_ada):
    d, n = w_ada.shape
    tn = 1024
    return pl.pallas_call(
        _ada_kernel,
        grid=(n // tn,),
        in_specs=[pl.BlockSpec((d, 1), lambda j: (0, 0)),
                  pl.BlockSpec((d, tn), lambda j: (0, j)),
                  pl.BlockSpec((1, tn), lambda j: (0, j))],
        out_specs=pl.BlockSpec((1, tn), lambda j: (0, j)),
        out_shape=jax.ShapeDtypeStruct((1, n), F32),
        compiler_params=_params("arbitrary"),
        name="ada_mod",
    )(c_col, w_ada, b_ada)


def _rope_kernel(pos_ref, invf_ref, sign_ref, wcast_ref, cos_ref, sin_ref, wcast_out_ref):
    ang = pos_ref[...].astype(F32) * invf_ref[...]
    cos_ref[...] = jnp.cos(ang)
    sin_ref[...] = jnp.sin(ang) * sign_ref[...]
    wcast_out_ref[...] = wcast_ref[...].astype(BF16)


def _cast_rider_specs(w, nsteps, step_of):
    rows, cols = w.shape
    tr = rows // nsteps
    assert tr * nsteps == rows and tr % 16 == 0, (rows, nsteps)
    spec = pl.BlockSpec((tr, cols), lambda *idx: (step_of(*idx), 0))
    return spec, spec, jax.ShapeDtypeStruct((rows, cols), BF16)


def _rope_tables(pos_col, invf, sign, w_cast):
    s = pos_col.shape[0]
    tm = min(s, 1024)
    cast_in, cast_out, cast_shape = _cast_rider_specs(w_cast, s // tm, lambda i: i)
    return pl.pallas_call(
        _rope_kernel,
        grid=(s // tm,),
        in_specs=[pl.BlockSpec((tm, 1), lambda i: (i, 0)),
                  pl.BlockSpec((1, LANES), lambda i: (0, 0)),
                  pl.BlockSpec((1, LANES), lambda i: (0, 0)),
                  cast_in],
        out_specs=[pl.BlockSpec((tm, LANES), lambda i: (i, 0)),
                   pl.BlockSpec((tm, LANES), lambda i: (i, 0)),
                   cast_out],
        out_shape=[jax.ShapeDtypeStruct((s, LANES), F32)] * 2 + [cast_shape],
        compiler_params=_params("arbitrary"),
        name="rope_tables",
    )(pos_col, invf, sign, w_cast)


SRC_LR = 2 * GLA_QK + 2 * GLA_WIDTH
SRC_AQ = SRC_LR + GLA_RANK
SRC_AV = SRC_AQ + 2 * ATT_WIDTH


def _wprep_kernel(w_ref, lrsrc_ref, main_ref, lr_ref):
    main_ref[...] = w_ref[...].astype(BF16)

    @pl.when(pl.program_id(0) == 0)
    def _():
        lr_ref[0:GLA_RANK, :] = lrsrc_ref[...].astype(BF16)
        lr_ref[GLA_RANK:, :] = jnp.zeros((LANES - GLA_RANK, lr_ref.shape[1]), BF16)


def _wprep(w_t):
    n, d = w_t.shape
    tr = 256

    def src_row(i):
        r = i * (tr // 8)
        tile = jnp.where(r < COL_GQ // 8, SRC_AQ // 8 + r,
                         jnp.where(r < COL_AV // 8, r - COL_GQ // 8, (SRC_AV - COL_AV) // 8 + r))
        return tile * 8

    return pl.pallas_call(
        _wprep_kernel,
        grid=(PROJ_COLS // tr,),
        in_specs=[pl.BlockSpec((pl.Element(tr), pl.Element(d)), lambda i: (src_row(i), 0)),
                  pl.BlockSpec((pl.Element(GLA_RANK), pl.Element(d)), lambda i: (SRC_LR, 0))],
        out_specs=[pl.BlockSpec((tr, d), lambda i: (i, 0)),
                   pl.BlockSpec((LANES, d), lambda i: (0, 0))],
        out_shape=[jax.ShapeDtypeStruct((PROJ_COLS, d), BF16),
                   jax.ShapeDtypeStruct((LANES, d), BF16)],
        compiler_params=_params("arbitrary"),
        name="w_in_prep",
    )(w_t, w_t)


def _inproj_kernel(x_ref, sc_ref, sh_ref, w_ref, wlr_ref, cos_ref, sin_ref,
                   proj_ref, glr_ref, u_ref, *, tn):
    j = pl.program_id(1)

    @pl.when(j == 0)
    def _():
        u = (x_ref[...] * (1.0 + sc_ref[...]) + sh_ref[...]).astype(BF16)
        u_ref[...] = u
        glr_ref[...] = _dot_nt(u, wlr_ref[...])

    col = j * tn
    q_scale = ATT_HD ** -0.5

    def sub_dot(n):
        return _dot_nt(u_ref[...], w_ref[n:n + PROJ_SUB, :])

    @pl.when(col < COL_GQ)
    def _():
        cos = cos_ref[...]
        sin = sin_ref[...]
        scale = jnp.where(col < COL_AK, q_scale, 1.0).astype(F32)
        for n in range(0, tn, PROJ_SUB):
            acc = sub_dot(n) * scale
            for h in range(n, n + PROJ_SUB, ATT_HD):
                t = acc[:, h - n:h - n + ATT_HD]
                proj_ref[:, h:h + ATT_HD] = t * cos + pltpu.roll(t, ATT_HD // 2, axis=1) * sin

    @pl.when(col >= COL_GQ)
    def _():
        for n in range(0, tn, PROJ_SUB):
            acc = sub_dot(n)
            if n < GLA_QK:
                acc = acc * jnp.where(col == COL_GQ, q_scale, 1.0).astype(F32)
            proj_ref[:, n:n + PROJ_SUB] = acc


def _inproj(x, sc, sh, w_main, w_lr, cos, sin):
    s, d = x.shape
    tm = min(s, 1024)
    tn = PROJ_TN
    return pl.pallas_call(
        functools.partial(_inproj_kernel, tn=tn),
        grid=(s // tm, PROJ_COLS // tn),
        in_specs=[pl.BlockSpec((tm, d), lambda i, j: (i, 0)),
                  pl.BlockSpec((1, d), lambda i, j: (0, 0)),
                  pl.BlockSpec((1, d), lambda i, j: (0, 0)),
                  pl.BlockSpec((tn, d), lambda i, j: (j, 0)),
                  pl.BlockSpec((LANES, d), lambda i, j: (0, 0)),
                  pl.BlockSpec((tm, LANES), lambda i, j: (i, 0)),
                  pl.BlockSpec((tm, LANES), lambda i, j: (i, 0))],
        out_specs=[pl.BlockSpec((tm, tn), lambda i, j: (i, j)),
                   pl.BlockSpec((tm, LANES), lambda i, j: (i, 0))],
        out_shape=[jax.ShapeDtypeStruct((s, PROJ_COLS), F32),
                   jax.ShapeDtypeStruct((s, LANES), F32)],
        scratch_shapes=[pltpu.VMEM((tm, d), BF16)],
        compiler_params=_params("arbitrary", "arbitrary"),
        name="in_proj",
    )(x, sc, sh, w_main, w_lr, cos, sin)


def _gla_kernel(q_ref, k_ref, v_ref, gr_ref, glr_ref, wg_ref, bg_ref, gn_ref, wcast_ref,
                o_ref, wcast_out_ref, state_ref):
    c = pl.program_id(0)
    C = GLA_CHUNK
    wcast_out_ref[...] = wcast_ref[...].astype(BF16)

    @pl.when(c == 0)
    def _():
        state_ref[...] = jnp.zeros_like(state_ref)

    row = lax.broadcasted_iota(jnp.int32, (C, C), 0)
    col = lax.broadcasted_iota(jnp.int32, (C, C), 1)

    g_hi, g_lo = _split_bf16(glr_ref[...])
    w_hi, w_lo = _split_bf16(wg_ref[...])
    z = _dot(g_hi, w_hi) + _dot(g_hi, w_lo) + _dot(g_lo, w_hi) + bg_ref[...]
    log_a = (jnp.minimum(z, 0.0) - jnp.log1p(jnp.exp(-jnp.abs(z)))) * (1.0 / GLA_TAU)

    tri = (col <= row).astype(BF16)
    a_hi, a_lo = _split_bf16(log_a)
    bcum = _dot(tri, a_hi) + _dot(tri, a_lo)

    nlev = len(GLA_LEVELS)
    r4 = lax.broadcasted_iota(jnp.int32, (nlev * C, C), 0)
    c4 = lax.broadcasted_iota(jnp.int32, (nlev * C, C), 1)
    sel = jnp.zeros((nlev * C, C), F32)
    for li, m in enumerate(GLA_LEVELS):
        t = r4 - li * C
        hit = (t >= 0) & (t < C) & (c4 == (t // (2 * m)) * (2 * m) + m - 1)
        sel = jnp.where(hit, 1.0, sel)
    ref_all = _dot(sel.astype(BF16), bcum.astype(BF16))

    g3 = lax.broadcasted_iota(jnp.int32, (C // GLA_DIAG, GLA_DIAG, C), 0)
    t3 = lax.broadcasted_iota(jnp.int32, (C // GLA_DIAG, GLA_DIAG, C), 1)
    s3 = lax.broadcasted_iota(jnp.int32, (C // GLA_DIAG, GLA_DIAG, C), 2)

    for h in range(GLA_HEADS):
        ks = slice(h * GLA_DK, (h + 1) * GLA_DK)
        vs = slice(h * GLA_DV, (h + 1) * GLA_DV)
        q = q_ref[:, ks]
        k = k_ref[:, ks]
        vb = v_ref[:, vs].astype(BF16)
        b = bcum[:, ks]
        b_last = b[C - 1:C, :]
        state = state_ref[h]

        o = _dot((q * jnp.exp(b)).astype(BF16), state.astype(BF16))

        scores = jnp.zeros((C, C), F32)
        for li, m in enumerate(GLA_LEVELS):
            ref = ref_all[li * C:(li + 1) * C, ks]
            upper = (row & m) != 0
            qm = q * jnp.exp(jnp.where(upper, b - ref, NEG_BIG))
            km = k * jnp.exp(jnp.where(upper, NEG_BIG, ref - b))
            part = _dot_nt(qm.astype(BF16), km.astype(BF16))
            same = (row // (2 * m)) == (col // (2 * m))
            scores = scores + jnp.where(same, part, 0.0)

        q3 = q.reshape(C // GLA_DIAG, GLA_DIAG, GLA_DK)
        k3 = k.reshape(C // GLA_DIAG, GLA_DIAG, GLA_DK)
        b3 = b.reshape(C // GLA_DIAG, GLA_DIAG, GLA_DK)
        diag = jnp.zeros((C // GLA_DIAG, GLA_DIAG, C), F32)
        for jj in range(GLA_DIAG):
            kj = k3[:, jj:jj + 1, :]
            bj = b3[:, jj:jj + 1, :]
            term = q3 * kj * jnp.exp(jnp.minimum(b3 - bj, 0.0))
            colsum = jnp.sum(term, axis=-1, keepdims=True)
            hit = (s3 == g3 * GLA_DIAG + jj) & (t3 >= jj)
            diag = diag + jnp.where(hit, colsum, 0.0)
        scores = scores + diag.reshape(C, C)

        o = o + _dot(scores.astype(BF16), vb)

        kl = (k * jnp.exp(b_last - b)).astype(BF16)
        decay_col = jnp.transpose(jnp.broadcast_to(jnp.exp(b_last), (C, GLA_DK)))
        decay = jnp.concatenate([decay_col, decay_col], axis=1)
        state_ref[h] = state * decay + _dot_tn(kl, vb)

        rs = lax.rsqrt(jnp.mean(o * o, axis=-1, keepdims=True) + RMS_EPS)
        gate = gr_ref[:, vs]
        gate = gate / (1.0 + jnp.exp(-gate))
        o_ref[:, vs] = (o * rs * gn_ref[...] * gate).astype(o_ref.dtype)


def _gla(proj, glr, wg_pad, bg, gnorm, w_cast):
    s = proj.shape[0]
    C = GLA_CHUNK
    cast_in, cast_out, cast_shape = _cast_rider_specs(w_cast, s // C, lambda c: c)
    return pl.pallas_call(
        _gla_kernel,
        grid=(s // C,),
        in_specs=[pl.BlockSpec((C, GLA_QK), lambda c: (c, COL_GQ // GLA_QK)),
                  pl.BlockSpec((C, GLA_QK), lambda c: (c, COL_GK // GLA_QK)),
                  pl.BlockSpec((C, GLA_WIDTH), lambda c: (c, COL_GV // GLA_WIDTH)),
                  pl.BlockSpec((C, GLA_WIDTH), lambda c: (c, COL_GR // GLA_WIDTH)),
                  pl.BlockSpec((C, LANES), lambda c: (c, 0)),
                  pl.BlockSpec((LANES, GLA_QK), lambda c: (0, 0)),
                  pl.BlockSpec((1, GLA_QK), lambda c: (0, 0)),
                  pl.BlockSpec((1, GLA_DV), lambda c: (0, 0)),
                  cast_in],
        out_specs=[pl.BlockSpec((C, GLA_WIDTH), lambda c: (c, 0)), cast_out],
        out_shape=[jax.ShapeDtypeStruct((s, GLA_WIDTH), BF16), cast_shape],
        scratch_shapes=[pltpu.VMEM((GLA_HEADS, GLA_DK, GLA_DV), F32)],
        compiler_params=_params("arbitrary"),
        name="gla",
    )(proj, proj, proj, proj, glr, wg_pad, bg, gnorm, w_cast)


def _att_kernel(q_ref, k_ref, v_ref, wcast_ref, o_ref, wcast_out_ref,
                q1, q4, q16, k1, k4, k16, v1, v4, v16,
                qs_ref, ks_ref, vs_ref, bias_ref, acc_ref, lse_ref):
    n = pl.program_id(1)
    wcast_out_ref[...] = wcast_ref[...].astype(BF16)
    B = ATT_SPAN
    T = ATT_SUPER
    qd, kd, vd = (q1, q4, q16), (k1, k4, k16), (v1, v4, v16)

    row = lax.broadcasted_iota(jnp.int32, (B, 2 * B), 0)
    col = lax.broadcasted_iota(jnp.int32, (B, 2 * B), 1)
    band = (col >= row) & (col <= row + B)
    bias_ref[0] = jnp.where(band, 0.0, NEG_BIG)
    bias_ref[1] = jnp.where(band & (col >= B), 0.0, NEG_BIG)

    @pl.when(n == 0)
    def _():
        for d, kb, vb in zip(ATT_DILATIONS, kd, vd):
            pitch = T // d + B
            for r in range(d):
                kb[r * pitch:r * pitch + B, :] = jnp.zeros((B, ATT_HD), BF16)
                vb[r * pitch:r * pitch + B, :] = jnp.zeros((B, ATT_HD), BF16)

    @pl.when(n > 0)
    def _():
        for d, kb, vb in zip(ATT_DILATIONS, kd, vd):
            pitch = T // d + B
            for r in range(d):
                kb[r * pitch:r * pitch + B, :] = kb[(r + 1) * pitch - B:(r + 1) * pitch, :]
                vb[r * pitch:r * pitch + B, :] = vb[(r + 1) * pitch - B:(r + 1) * pitch, :]

    assert ATT_DILATIONS == (1, 4, 16)
    for src, stage, bufs, front in ((q_ref, qs_ref, qd, 0), (k_ref, ks_ref, kd, B), (v_ref, vs_ref, vd, B)):
        b1, b4, b16 = bufs
        b1[front:front + T, :] = src[...].astype(BF16)
        L4, L16 = T // 4, T // 16
        for r in range(4):
            val = src[pl.ds(r, L4, stride=4), :]
            stage[r * L4:(r + 1) * L4, :] = val
            lo = r * (L4 + front) + front
            b4[lo:lo + L4, :] = val.astype(BF16)
        for r in range(16):
            val = stage[pl.ds((r % 4) * L4 + r // 4, L16, stride=4), :]
            lo = r * (L16 + front) + front
            b16[lo:lo + L16, :] = val.astype(BF16)

    ones = jnp.ones((2 * B, ATT_HD), BF16)

    for bi, (d, qb, kb, vb) in enumerate(zip(ATT_DILATIONS, qd, kd, vd)):
        nblk = T // (B * d)
        pitch = T // d + B

        def body(it, carry, bi=bi, d=d, qb=qb, kb=kb, vb=vb, nblk=nblk, pitch=pitch):
            r = it // nblk
            blk = it % nblk
            qs = qb[pl.ds(pl.multiple_of(it * B, B), B), :]
            kstart = pl.multiple_of(r * pitch + blk * B, B)
            kcat = kb[pl.ds(kstart, 2 * B), :]
            vcat = vb[pl.ds(kstart, 2 * B), :]
            no_prev = jnp.where((n == 0) & (blk == 0), 1, 0)
            s = _dot_nt(qs, kcat) + bias_ref[no_prev]
            m = jnp.max(s, axis=1, keepdims=True)
            p = jnp.exp(s - m).astype(BF16)
            pv = _dot(p, jnp.concatenate([vcat, ones], axis=1))
            start = r + blk * B * d
            dst = pl.ds(start, B, stride=d) if d > 1 else pl.ds(pl.multiple_of(start, B), B)
            l = pv[:, ATT_HD:]
            acc_ref[bi, dst, :] = pv[:, :ATT_HD] / l
            lse_ref[bi, dst, :] = m + jnp.log(l)
            return carry

        lax.fori_loop(0, T // B, body, 0, unroll=16)

    s0, s1, s2 = lse_ref[0], lse_ref[1], lse_ref[2]
    mm = jnp.maximum(jnp.maximum(s0, s1), s2)
    c0, c1, c2 = jnp.exp(s0 - mm), jnp.exp(s1 - mm), jnp.exp(s2 - mm)
    num = c0 * acc_ref[0] + c1 * acc_ref[1] + c2 * acc_ref[2]
    o_ref[...] = (num / (c0 + c1 + c2)).astype(o_ref.dtype)


def _attention(proj, w_cast):
    s = proj.shape[0]
    T = ATT_SUPER
    ns = s // T
    qc, kc, vc = COL_AQ // ATT_HD, COL_AK // ATT_HD, COL_AV // ATT_HD
    blk = (T, ATT_HD)
    nb = len(ATT_DILATIONS)
    q_bufs = [pltpu.VMEM((T, ATT_HD), BF16) for _ in ATT_DILATIONS]
    kv_bufs = [pltpu.VMEM((T + d * ATT_SPAN, ATT_HD), BF16) for d in ATT_DILATIONS]
    cast_in, cast_out, cast_shape = _cast_rider_specs(w_cast, ATT_HEADS * ns, lambda h, n: h * ns + n)
    return pl.pallas_call(
        _att_kernel,
        grid=(ATT_HEADS, ns),
        in_specs=[pl.BlockSpec(blk, lambda h, n: (n, qc + h)),
                  pl.BlockSpec(blk, lambda h, n: (n, kc + h)),
                  pl.BlockSpec(blk, lambda h, n: (n, vc + h)),
                  cast_in],
        out_specs=[pl.BlockSpec(blk, lambda h, n: (n, h)), cast_out],
        out_shape=[jax.ShapeDtypeStruct((s, ATT_WIDTH), BF16), cast_shape],
        scratch_shapes=q_bufs + kv_bufs + kv_bufs
        + [pltpu.VMEM((T, ATT_HD), F32)] * 3
        + [pltpu.VMEM((2, ATT_SPAN, 2 * ATT_SPAN), F32)]
        + [pltpu.VMEM((nb, T, LANES), F32)] * 2,
        compiler_params=_params("arbitrary", "arbitrary"),
        name="dilated_attn",
    )(proj, proj, proj, w_cast)


def _layer_norm(z, g, b):
    mu = jnp.mean(z, axis=-1, keepdims=True)
    zc = z - mu
    var = jnp.mean(zc * zc, axis=-1, keepdims=True)
    return zc * lax.rsqrt(var + LN_EPS) * g + b


def _outproj_kernel(og_ref, oa_ref, wg_ref, wa_ref, x_ref, gate_ref, lg_ref, lb_ref, o_ref):
    y = _dot(og_ref[...], wg_ref[...]) + _dot(oa_ref[...], wa_ref[...])
    z = DEEPNORM_ALPHA * x_ref[...] + (1.0 + gate_ref[...]) * y
    o_ref[...] = _layer_norm(z, lg_ref[...], lb_ref[...])


def _outproj(og, oa, wo, x, gate, ln_g, ln_b):
    s, d = x.shape
    tm = min(s, 512)
    vec = pl.BlockSpec((1, d), lambda i: (0, 0))
    return pl.pallas_call(
        _outproj_kernel,
        grid=(s // tm,),
        in_specs=[pl.BlockSpec((tm, GLA_WIDTH), lambda i: (i, 0)),
                  pl.BlockSpec((tm, ATT_WIDTH), lambda i: (i, 0)),
                  pl.BlockSpec((GLA_WIDTH, d), lambda i: (0, 0)),
                  pl.BlockSpec((ATT_WIDTH, d), lambda i: (GLA_WIDTH // ATT_WIDTH, 0)),
                  pl.BlockSpec((tm, d), lambda i: (i, 0)),
                  vec, vec, vec],
        out_specs=pl.BlockSpec((tm, d), lambda i: (i, 0)),
        out_shape=jax.ShapeDtypeStruct((s, d), F32),
        compiler_params=_params("arbitrary"),
        name="out_proj_ln",
    )(og, oa, wo, wo, x, gate, ln_g, ln_b)


def _ffn_kernel(x_ref, sc_ref, sh_ref, gate_ref, wa_ref, wg_ref, cwa_ref, cwg_ref,
                cba_ref, cbg_ref, wd_ref, lg_ref, lb_ref, o_ref,
                u_ref, ha_ref, hg_ref, ta_ref, tg_ref, *, tm, nf):
    i = pl.program_id(0)
    j = pl.program_id(1)
    H = 8
    acc_ref = o_ref

    @pl.when(j == 0)
    def _():
        u_ref[...] = (x_ref[...] * (1.0 + sc_ref[...]) + sh_ref[...]).astype(BF16)
        acc_ref[...] = jnp.zeros_like(acc_ref)

    @pl.when(i == 0)
    def _():
        ta_ref[j] = jnp.zeros(ta_ref.shape[1:], F32)
        tg_ref[j] = jnp.zeros(tg_ref.shape[1:], F32)

    ha_ref[0:H, :] = ta_ref[j]
    hg_ref[0:H, :] = tg_ref[j]

    R = min(tm, FFN_ROWS)

    def up(r):
        u = u_ref[r:r + R, :]
        ha = _dot(u, wa_ref[...])
        hg = _dot(u, wg_ref[...])
        ha_ref[H + r:H + r + R, :] = ha
        hg_ref[H + r:H + r + R, :] = hg
        return ha, hg

    def conv(h, cw_ref, cb_ref, h_ref, r):
        cw = cw_ref[...]
        return (cb_ref[...] + cw[0:1, :] * h_ref[H + r - 2:H + r - 2 + R, :]
                + cw[1:2, :] * h_ref[H + r - 1:H + r - 1 + R, :] + cw[2:3, :] * h)

    nxt = up(0)
    for r in range(0, tm, R):
        ha, hg = nxt
        if r + R < tm:
            nxt = up(r + R)
        ya = conv(ha, cwa_ref, cba_ref, ha_ref, r)
        yg = conv(hg, cwg_ref, cbg_ref, hg_ref, r)
        act = (yg / (1.0 + jnp.exp(-yg)) * ya).astype(BF16)
        acc_ref[r:r + R, :] += _dot(act, wd_ref[...])

    ta_ref[j] = ha_ref[tm:tm + H, :]
    tg_ref[j] = hg_ref[tm:tm + H, :]

    @pl.when(j == nf - 1)
    def _():
        z = DEEPNORM_ALPHA * x_ref[...] + (1.0 + gate_ref[...]) * acc_ref[...]
        o_ref[...] = _layer_norm(z, lg_ref[...], lb_ref[...])


def _ffn(x, sc, sh, gate, w_up, conv_w, conv_b, w_down, ln_g, ln_b):
    s, d = x.shape
    f = w_down.shape[0]
    tm = min(s, 1024)
    fc = 512
    nf = f // fc
    vec = pl.BlockSpec((1, d), lambda i, j: (0, 0))
    return pl.pallas_call(
        functools.partial(_ffn_kernel, tm=tm, nf=nf),
        grid=(s // tm, nf),
        in_specs=[pl.BlockSpec((tm, d), lambda i, j: (i, 0), pipeline_mode=pl.Buffered(1)),
                  vec, vec, vec,
                  pl.BlockSpec((d, fc), lambda i, j: (0, j)),
                  pl.BlockSpec((d, fc), lambda i, j: (0, j + nf)),
                  pl.BlockSpec((3, fc), lambda i, j: (0, j)),
                  pl.BlockSpec((3, fc), lambda i, j: (0, j + nf)),
                  pl.BlockSpec((1, fc), lambda i, j: (0, j)),
                  pl.BlockSpec((1, fc), lambda i, j: (0, j + nf)),
                  pl.BlockSpec((fc, d), lambda i, j: (j, 0)),
                  vec, vec],
        out_specs=pl.BlockSpec((tm, d), lambda i, j: (i, 0)),
        out_shape=jax.ShapeDtypeStruct((s, d), F32),
        scratch_shapes=[pltpu.VMEM((tm, d), BF16),
                        pltpu.VMEM((tm + 8, fc), F32),
                        pltpu.VMEM((tm + 8, fc), F32),
                        pltpu.VMEM((nf, 8, fc), F32),
                        pltpu.VMEM((nf, 8, fc), F32)],
        compiler_params=_params("arbitrary", "arbitrary"),
        name="conv_ffn_ln",
    )(x, sc, sh, gate, w_up, w_up, conv_w, conv_w, conv_b, conv_b, w_down, ln_g, ln_b)


def kernel(x, c, positions, w_ada, b_ada, w_in, w_gla_gate, b_gla_gate, gla_norm_g, w_out,
           ln1_g, ln1_b, w_up, conv_w, conv_b, w_down, ln2_g, ln2_b):
    batch, s, d = x.shape
    half = ATT_HD // 2
    inv_freq = ROPE_THETA ** (-jnp.arange(half, dtype=F32) / half)
    invf = jnp.concatenate([inv_freq, inv_freq]).reshape(1, ATT_HD)
    sign = jnp.concatenate([-jnp.ones((half,), F32), jnp.ones((half,), F32)]).reshape(1, ATT_HD)
    outs = []
    for bi in range(batch):
        xb = x.reshape(s, d) if batch == 1 else x[bi]
        for layer in range(w_in.shape[0]):
            mod = _ada(c[bi].reshape(d, 1), w_ada[layer], b_ada[layer].reshape(1, -1))
            sh1, sc1, g1, sh2, sc2, g2 = [mod[:, k * d:(k + 1) * d] for k in range(6)]
            w_main, w_lr = _wprep(jnp.transpose(w_in[layer]))
            wg_pad = jnp.pad(w_gla_gate[layer], ((0, LANES - GLA_RANK), (0, 0)))
            cos, sin, wo = _rope_tables(positions[bi].reshape(s, 1), invf, sign, w_out[layer])
            proj, glr = _inproj(xb, sc1, sh1, w_main, w_lr, cos, sin)
            og, w_up_b = _gla(proj, glr, wg_pad, b_gla_gate[layer].reshape(1, -1),
                              gla_norm_g[layer].reshape(1, -1), w_up[layer])
            oa, w_down_b = _attention(proj, w_down[layer])
            xb = _outproj(og, oa, wo, xb, g1,
                          ln1_g[layer].reshape(1, -1), ln1_b[layer].reshape(1, -1))
            xb = _ffn(xb, sc2, sh2, g2, w_up_b, conv_w[layer],
                      conv_b[layer].reshape(1, -1), w_down_b,
                      ln2_g[layer].reshape(1, -1), ln2_b[layer].reshape(1, -1))
        outs.append(xb)
    return outs[0].reshape(1, s, d) if batch == 1 else jnp.stack(outs, axis=0)
```

```python
import functools

import jax
import jax.numpy as jnp
from jax import lax
from jax.experimental import pallas as pl
from jax.experimental.pallas import tpu as pltpu

F32 = jnp.float32
BF16 = jnp.bfloat16

D_MODEL = 2048
GLA_HEADS = 4
GLA_DK = 128
GLA_DV = 256
GLA_WIDTH = GLA_HEADS * GLA_DV
GLA_QK = GLA_HEADS * GLA_DK
GLA_RANK = 16
GLA_TAU = 16.0
ATT_HEADS = 8
ATT_HD = 128
ATT_WIDTH = ATT_HEADS * ATT_HD
ATT_SPAN = 128
ATT_DILATIONS = (1, 4, 16)
D_FF = 5632
ROPE_THETA = 10000.0
DEEPNORM_ALPHA = 2.0 ** 0.25
LN_EPS = 1e-5
RMS_EPS = 1e-6
LOG2_E = 1.4426950408889634

LANES = 128
GLA_CHUNK = 128
GLA_STEP_CHUNKS = 4
GLA_LEVELS = (64, 32, 16, 8)
GLA_DIAG = 8
ATT_SUPER = ATT_SPAN * max(ATT_DILATIONS)
FFN_ROWS = 512
NEG_BIG = -1e30
VMEM_LIMIT = 56 * 1024 * 1024

COL_AQ, COL_AK = 0, 1024
COL_GQ, COL_GK, COL_GV, COL_GR = 2048, 2560, 3072, 4096
COL_AV = 5120
PROJ_COLS = 6144
PROJ_TN = 1024
PROJ_SUB = 512


def _dot(a, b):
    return jnp.dot(a, b, preferred_element_type=F32)


def _dot_nt(a, b):
    return lax.dot_general(a, b, (((1,), (1,)), ((), ())), preferred_element_type=F32)


def _dot_tn(a, b):
    return lax.dot_general(a, b, (((0,), (0,)), ((), ())), preferred_element_type=F32)


def _split_bf16(a):
    hi = a.astype(BF16)
    lo = (a - hi.astype(F32)).astype(BF16)
    return hi, lo


def _params(*sem):
    return pltpu.CompilerParams(dimension_semantics=sem, vmem_limit_bytes=VMEM_LIMIT)


def _ada_kernel(c_ref, w_ref, b_ref, o_ref):
    cc = c_ref[...]
    s = cc / (1.0 + jnp.exp(-cc))
    o_ref[...] = jnp.sum(s * w_ref[...], axis=0, keepdims=True) + b_ref[...]


def _ada(c_col, w_ada, b_ada):
    d, n = w_ada.shape
    tn = 1024
    return pl.pallas_call(
        _ada_kernel,
        grid=(n // tn,),
        in_specs=[pl.BlockSpec((d, 1), lambda j: (0, 0)),
                  pl.BlockSpec((d, tn), lambda j: (0, j)),
                  pl.BlockSpec((1, tn), lambda j: (0, j))],
        out_specs=pl.BlockSpec((1, tn), lambda j: (0, j)),
        out_shape=jax.ShapeDtypeStruct((1, n), F32),
        compiler_params=_params("arbitrary"),
        name="ada_mod",
    )(c_col, w_ada, b_ada)


def _rope_kernel(pos_ref, invf_ref, sign_ref, wcast_ref, cos_ref, sin_ref, wcast_out_ref):
    ang = pos_ref[...].astype(F32) * invf_ref[...]
    cos_ref[...] = jnp.cos(ang)
    sin_ref[...] = jnp.sin(ang) * sign_ref[...]
    wcast_out_ref[...] = wcast_ref[...].astype(BF16)


def _cast_rider_specs(w, nsteps, step_of):
    rows, cols = w.shape
    tr = rows // nsteps
    assert tr * nsteps == rows and tr % 16 == 0, (rows, nsteps)
    spec = pl.BlockSpec((tr, cols), lambda *idx: (step_of(*idx), 0))
    return spec, spec, jax.ShapeDtypeStruct((rows, cols), BF16)


def _rope_tables(pos_col, invf, sign, w_cast):
    s = pos_col.shape[0]
    tm = min(s, 1024)
    cast_in, cast_out, cast_shape = _cast_rider_specs(w_cast, s // tm, lambda i: i)
    return pl.pallas_call(
        _rope_kernel,
        grid=(s // tm,),
        in_specs=[pl.BlockSpec((tm, 1), lambda i: (i, 0)),
                  pl.BlockSpec((1, LANES), lambda i: (0, 0)),
                  pl.BlockSpec((1, LANES), lambda i: (0, 0)),
                  cast_in],
        out_specs=[pl.BlockSpec((tm, LANES), lambda i: (i, 0)),
                   pl.BlockSpec((tm, LANES), lambda i: (i, 0)),
                   cast_out],
        out_shape=[jax.ShapeDtypeStruct((s, LANES), F32)] * 2 + [cast_shape],
        compiler_params=_params("arbitrary"),
        name="rope_tables",
    )(pos_col, invf, sign, w_cast)


SRC_LR = 2 * GLA_QK + 2 * GLA_WIDTH
SRC_AQ = SRC_LR + GLA_RANK
SRC_AV = SRC_AQ + 2 * ATT_WIDTH


def _wprep_kernel(w_ref, lrsrc_ref, main_ref, lr_ref):
    main_ref[...] = w_ref[...].astype(BF16)

    @pl.when(pl.program_id(0) == 0)
    def _():
        lr_ref[0:GLA_RANK, :] = lrsrc_ref[...].astype(BF16)
        lr_ref[GLA_RANK:, :] = jnp.zeros((LANES - GLA_RANK, lr_ref.shape[1]), BF16)


def _wprep(w_t):
    n, d = w_t.shape
    tr = 256

    def src_row(i):
        r = i * (tr // 8)
        tile = jnp.where(r < COL_GQ // 8, SRC_AQ // 8 + r,
                         jnp.where(r < COL_AV // 8, r - COL_GQ // 8, (SRC_AV - COL_AV) // 8 + r))
        return tile * 8

    return pl.pallas_call(
        _wprep_kernel,
        grid=(PROJ_COLS // tr,),
        in_specs=[pl.BlockSpec((pl.Element(tr), pl.Element(d)), lambda i: (src_row(i), 0)),
                  pl.BlockSpec((pl.Element(GLA_RANK), pl.Element(d)), lambda i: (SRC_LR, 0))],
        out_specs=[pl.BlockSpec((tr, d), lambda i: (i, 0)),
                   pl.BlockSpec((LANES, d), lambda i: (0, 0))],
        out_shape=[jax.ShapeDtypeStruct((PROJ_COLS, d), BF16),
                   jax.ShapeDtypeStruct((LANES, d), BF16)],
        compiler_params=_params("arbitrary"),
        name="w_in_prep",
    )(w_t, w_t)


def _inproj_kernel(x_ref, sc_ref, sh_ref, w_ref, wlr_ref, cos_ref, sin_ref,
                   proj_ref, glr_ref, u_ref, *, tn):
    j = pl.program_id(1)

    @pl.when(j == 0)
    def _():
        u = (x_ref[...] * (1.0 + sc_ref[...]) + sh_ref[...]).astype(BF16)
        u_ref[...] = u
        glr_ref[...] = _dot_nt(u, wlr_ref[...])

    col = j * tn
    q_scale = ATT_HD ** -0.5

    def sub_dot(n):
        return _dot_nt(u_ref[...], w_ref[n:n + PROJ_SUB, :])

    @pl.when(col < COL_GQ)
    def _():
        cos = cos_ref[...]
        sin = sin_ref[...]
        scale = jnp.where(col < COL_AK, q_scale, 1.0).astype(F32)
        for n in range(0, tn, PROJ_SUB):
            acc = sub_dot(n) * scale
            for h in range(n, n + PROJ_SUB, ATT_HD):
                t = acc[:, h - n:h - n + ATT_HD]
                proj_ref[:, h:h + ATT_HD] = t * cos + pltpu.roll(t, ATT_HD // 2, axis=1) * sin

    @pl.when(col >= COL_GQ)
    def _():
        for n in range(0, tn, PROJ_SUB):
            acc = sub_dot(n)
            if n < GLA_QK:
                acc = acc * jnp.where(col == COL_GQ, q_scale, 1.0).astype(F32)
            proj_ref[:, n:n + PROJ_SUB] = acc


def _inproj(x, sc, sh, w_main, w_lr, cos, sin):
    s, d = x.shape
    tm = min(s, 1024)
    tn = PROJ_TN
    return pl.pallas_call(
        functools.partial(_inproj_kernel, tn=tn),
        grid=(s // tm, PROJ_COLS // tn),
        in_specs=[pl.BlockSpec((tm, d), lambda i, j: (i, 0)),
                  pl.BlockSpec((1, d), lambda i, j: (0, 0)),
                  pl.BlockSpec((1, d), lambda i, j: (0, 0)),
                  pl.BlockSpec((tn, d), lambda i, j: (j, 0)),
                  pl.BlockSpec((LANES, d), lambda i, j: (0, 0)),
                  pl.BlockSpec((tm, LANES), lambda i, j: (i, 0)),
                  pl.BlockSpec((tm, LANES), lambda i, j: (i, 0))],
        out_specs=[pl.BlockSpec((tm, tn), lambda i, j: (i, j)),
                   pl.BlockSpec((tm, LANES), lambda i, j: (i, 0))],
        out_shape=[jax.ShapeDtypeStruct((s, PROJ_COLS), F32),
                   jax.ShapeDtypeStruct((s, LANES), F32)],
        scratch_shapes=[pltpu.VMEM((tm, d), BF16)],
        compiler_params=_params("arbitrary", "arbitrary"),
        name="in_proj",
    )(x, sc, sh, w_main, w_lr, cos, sin)


def _gla_kernel(q_ref, k_ref, v_ref, gr_ref, glr_ref, wgh_ref, wgl_ref, bg_ref, gn_ref, wcast_ref,
                o_ref, wcast_out_ref, state_ref):
    c = pl.program_id(0)
    C = GLA_CHUNK
    wcast_out_ref[...] = wcast_ref[...].astype(BF16)

    @pl.when(c == 0)
    def _():
        state_ref[...] = jnp.zeros_like(state_ref)

    row = lax.broadcasted_iota(jnp.int32, (C, C), 0)
    col = lax.broadcasted_iota(jnp.int32, (C, C), 1)

    tri = (col <= row).astype(BF16)

    nlev = len(GLA_LEVELS)
    r4 = lax.broadcasted_iota(jnp.int32, (nlev * C, C), 0)
    c4 = lax.broadcasted_iota(jnp.int32, (nlev * C, C), 1)
    sel = jnp.zeros((nlev * C, C), F32)
    for li, m in enumerate(GLA_LEVELS):
        t = r4 - li * C
        hit = (t >= 0) & (t < C) & (c4 == (t // (2 * m)) * (2 * m) + m - 1)
        sel = jnp.where(hit, 1.0, sel)
    sel = sel.astype(BF16)

    g3 = lax.broadcasted_iota(jnp.int32, (C // GLA_DIAG, GLA_DIAG, C), 0)
    t3 = lax.broadcasted_iota(jnp.int32, (C // GLA_DIAG, GLA_DIAG, C), 1)
    s3 = lax.broadcasted_iota(jnp.int32, (C // GLA_DIAG, GLA_DIAG, C), 2)

    for rows in [slice(r, r + C) for r in range(0, q_ref.shape[0], C)]:
        _gla_chunk(rows, q_ref, k_ref, v_ref, gr_ref, glr_ref, wgh_ref, wgl_ref, bg_ref, gn_ref,
                   o_ref, state_ref, row, col, tri, sel, g3, t3, s3)


def _gla_chunk(rows, q_ref, k_ref, v_ref, gr_ref, glr_ref, wgh_ref, wgl_ref, bg_ref, gn_ref,
               o_ref, state_ref, row, col, tri, sel, g3, t3, s3):
    C = GLA_CHUNK
    g_hi, g_lo = _split_bf16(glr_ref[rows, :])
    w_hi, w_lo = wgh_ref[...], wgl_ref[...]
    z = _dot(g_hi, w_hi) + _dot(g_hi, w_lo) + _dot(g_lo, w_hi) + bg_ref[...]
    log_a = (jnp.minimum(z, 0.0) - jnp.log1p(jnp.exp(-jnp.abs(z)))) * (LOG2_E / GLA_TAU)

    a_hi, a_lo = _split_bf16(log_a)
    bcum = _dot(tri, a_hi) + _dot(tri, a_lo)

    ref_all = _dot(sel, bcum.astype(BF16))

    for h in range(GLA_HEADS):
        ks = slice(h * GLA_DK, (h + 1) * GLA_DK)
        vs = slice(h * GLA_DV, (h + 1) * GLA_DV)
        q = q_ref[rows, ks]
        k = k_ref[rows, ks]
        vb = v_ref[rows, vs].astype(BF16)
        b = bcum[:, ks]
        b_last = b[C - 1:C, :]
        state = state_ref[h]

        o = _dot((q * jnp.exp2(b)).astype(BF16), state.astype(BF16))

        scores = jnp.zeros((C, C), F32)
        for li, m in enumerate(GLA_LEVELS):
            ref = ref_all[li * C:(li + 1) * C, ks]
            upper = (row & m) != 0
            qm = q * jnp.exp2(jnp.where(upper, b - ref, NEG_BIG))
            km = k * jnp.exp2(jnp.where(upper, NEG_BIG, ref - b))
            part = _dot_nt(qm.astype(BF16), km.astype(BF16))
            same = (row // (2 * m)) == (col // (2 * m))
            scores = scores + jnp.where(same, part, 0.0)

        groups = C // GLA_DIAG
        q3 = q.reshape(groups, GLA_DIAG, GLA_DK)
        k3 = k.reshape(groups, GLA_DIAG, GLA_DK)
        b3 = b.reshape(groups, GLA_DIAG, GLA_DK)
        diag = jnp.zeros((groups, GLA_DIAG, C), F32)
        for jj in range(GLA_DIAG):
            term = q3 * k3[:, jj:jj + 1, :] * jnp.exp2(b3 - b3[:, jj:jj + 1, :])
            colsum = jnp.sum(term, axis=-1, keepdims=True)
            hit = (s3 == g3 * GLA_DIAG + jj) & (t3 >= jj)
            diag = jnp.where(hit, colsum, diag)
        scores = scores + diag.reshape(C, C)

        o = o + _dot(scores.astype(BF16), vb)

        kl = (k * jnp.exp2(b_last - b)).astype(BF16)
        decay_col = jnp.transpose(jnp.broadcast_to(jnp.exp2(b_last), (C, GLA_DK)))
        decay = jnp.concatenate([decay_col, decay_col], axis=1)
        state_ref[h] = state * decay + _dot_tn(kl, vb)

        rs = lax.rsqrt(jnp.mean(o * o, axis=-1, keepdims=True) + RMS_EPS)
        gate = gr_ref[rows, vs]
        gate = gate / (1.0 + jnp.exp(-gate))
        o_ref[rows, vs] = (o * rs * gn_ref[...] * gate).astype(o_ref.dtype)


def _gla(proj, glr, wg_pad, bg, gnorm, w_cast):
    s = proj.shape[0]
    C = GLA_CHUNK * GLA_STEP_CHUNKS
    wg_hi, wg_lo = _split_bf16(wg_pad)
    cast_in, cast_out, cast_shape = _cast_rider_specs(w_cast, s // C, lambda c: c)
    return pl.pallas_call(
        _gla_kernel,
        grid=(s // C,),
        in_specs=[pl.BlockSpec((C, GLA_QK), lambda c: (c, COL_GQ // GLA_QK)),
                  pl.BlockSpec((C, GLA_QK), lambda c: (c, COL_GK // GLA_QK)),
                  pl.BlockSpec((C, GLA_WIDTH), lambda c: (c, COL_GV // GLA_WIDTH)),
                  pl.BlockSpec((C, GLA_WIDTH), lambda c: (c, COL_GR // GLA_WIDTH)),
                  pl.BlockSpec((C, LANES), lambda c: (c, 0)),
                  pl.BlockSpec((LANES, GLA_QK), lambda c: (0, 0)),
                  pl.BlockSpec((LANES, GLA_QK), lambda c: (0, 0)),
                  pl.BlockSpec((1, GLA_QK), lambda c: (0, 0)),
                  pl.BlockSpec((1, GLA_DV), lambda c: (0, 0)),
                  cast_in],
        out_specs=[pl.BlockSpec((C, GLA_WIDTH), lambda c: (c, 0)), cast_out],
        out_shape=[jax.ShapeDtypeStruct((s, GLA_WIDTH), BF16), cast_shape],
        scratch_shapes=[pltpu.VMEM((GLA_HEADS, GLA_DK, GLA_DV), F32)],
        compiler_params=_params("arbitrary"),
        name="gla",
    )(proj, proj, proj, proj, glr, wg_hi, wg_lo, bg, gnorm, w_cast)


def _att_kernel(q_ref, k_ref, v_ref, wcast_ref, o_ref, wcast_out_ref,
                q1, q4, q16, k1, k4, k16, v1, v4, v16,
                qs_ref, ks_ref, vs_ref, bias_ref, acc_ref, lse_ref):
    n = pl.program_id(1)
    wcast_out_ref[...] = wcast_ref[...].astype(BF16)
    B = ATT_SPAN
    T = ATT_SUPER
    qd, kd, vd = (q1, q4, q16), (k1, k4, k16), (v1, v4, v16)

    row = lax.broadcasted_iota(jnp.int32, (B, 2 * B), 0)
    col = lax.broadcasted_iota(jnp.int32, (B, 2 * B), 1)
    band = (col >= row) & (col <= row + B)
    bias_ref[0] = jnp.where(band, 0.0, NEG_BIG)
    bias_ref[1] = jnp.where(band & (col >= B), 0.0, NEG_BIG)

    @pl.when(n == 0)
    def _():
        for d, kb, vb in zip(ATT_DILATIONS, kd, vd):
            pitch = T // d + B
            for r in range(d):
                kb[r * pitch:r * pitch + B, :] = jnp.zeros((B, ATT_HD), BF16)
                vb[r * pitch:r * pitch + B, :] = jnp.zeros((B, ATT_HD), BF16)

    @pl.when(n > 0)
    def _():
        for d, kb, vb in zip(ATT_DILATIONS, kd, vd):
            pitch = T // d + B
            for r in range(d):
                kb[r * pitch:r * pitch + B, :] = kb[(r + 1) * pitch - B:(r + 1) * pitch, :]
                vb[r * pitch:r * pitch + B, :] = vb[(r + 1) * pitch - B:(r + 1) * pitch, :]

    assert ATT_DILATIONS == (1, 4, 16)
    for src, stage, bufs, front in ((q_ref, qs_ref, qd, 0), (k_ref, ks_ref, kd, B), (v_ref, vs_ref, vd, B)):
        b1, b4, b16 = bufs
        b1[front:front + T, :] = src[...].astype(BF16)
        L4, L16 = T // 4, T // 16
        for r in range(4):
            val = src[pl.ds(r, L4, stride=4), :]
            stage[r * L4:(r + 1) * L4, :] = val
            lo = r * (L4 + front) + front
            b4[lo:lo + L4, :] = val.astype(BF16)
        for r in range(16):
            val = stage[pl.ds((r % 4) * L4 + r // 4, L16, stride=4), :]
            lo = r * (L16 + front) + front
            b16[lo:lo + L16, :] = val.astype(BF16)

    ones = jnp.ones((2 * B, ATT_HD), BF16)

    for bi, (d, qb, kb, vb) in enumerate(zip(ATT_DILATIONS, qd, kd, vd)):
        nblk = T // (B * d)
        pitch = T // d + B

        def body(it, carry, bi=bi, d=d, qb=qb, kb=kb, vb=vb, nblk=nblk, pitch=pitch):
            r = it // nblk
            blk = it % nblk
            qs = qb[pl.ds(pl.multiple_of(it * B, B), B), :]
            kstart = pl.multiple_of(r * pitch + blk * B, B)
            kcat = kb[pl.ds(kstart, 2 * B), :]
            vcat = vb[pl.ds(kstart, 2 * B), :]
            no_prev = jnp.where((n == 0) & (blk == 0), 1, 0)
            s = _dot_nt(qs, kcat) + bias_ref[no_prev]
            m = jnp.max(s, axis=1, keepdims=True)
            p = jnp.exp(s - m).astype(BF16)
            pv = _dot(p, jnp.concatenate([vcat, ones], axis=1))
            start = r + blk * B * d
            dst = pl.ds(start, B, stride=d) if d > 1 else pl.ds(pl.multiple_of(start, B), B)
            l = pv[:, ATT_HD:]
            acc_ref[bi, dst, :] = pv[:, :ATT_HD] / l
            lse_ref[bi, dst, :] = m + jnp.log(l)
            return carry

        lax.fori_loop(0, T // B, body, 0, unroll=16)

    s0, s1, s2 = lse_ref[0], lse_ref[1], lse_ref[2]
    mm = jnp.maximum(jnp.maximum(s0, s1), s2)
    c0, c1, c2 = jnp.exp(s0 - mm), jnp.exp(s1 - mm), jnp.exp(s2 - mm)
    num = c0 * acc_ref[0] + c1 * acc_ref[1] + c2 * acc_ref[2]
    o_ref[...] = (num / (c0 + c1 + c2)).astype(o_ref.dtype)


def _attention(proj, w_cast):
    s = proj.shape[0]
    T = ATT_SUPER
    ns = s // T
    qc, kc, vc = COL_AQ // ATT_HD, COL_AK // ATT_HD, COL_AV // ATT_HD
    blk = (T, ATT_HD)
    nb = len(ATT_DILATIONS)
    q_bufs = [pltpu.VMEM((T, ATT_HD), BF16) for _ in ATT_DILATIONS]
    kv_bufs = [pltpu.VMEM((T + d * ATT_SPAN, ATT_HD), BF16) for d in ATT_DILATIONS]
    cast_in, cast_out, cast_shape = _cast_rider_specs(w_cast, ATT_HEADS * ns, lambda h, n: h * ns + n)
    return pl.pallas_call(
        _att_kernel,
        grid=(ATT_HEADS, ns),
        in_specs=[pl.BlockSpec(blk, lambda h, n: (n, qc + h)),
                  pl.BlockSpec(blk, lambda h, n: (n, kc + h)),
                  pl.BlockSpec(blk, lambda h, n: (n, vc + h)),
                  cast_in],
        out_specs=[pl.BlockSpec(blk, lambda h, n: (n, h)), cast_out],
        out_shape=[jax.ShapeDtypeStruct((s, ATT_WIDTH), BF16), cast_shape],
        scratch_shapes=q_bufs + kv_bufs + kv_bufs
        + [pltpu.VMEM((T, ATT_HD), F32)] * 3
        + [pltpu.VMEM((2, ATT_SPAN, 2 * ATT_SPAN), F32)]
        + [pltpu.VMEM((nb, T, LANES), F32)] * 2,
        compiler_params=_params("arbitrary", "arbitrary"),
        name="dilated_attn",
    )(proj, proj, proj, w_cast)


def _layer_norm(z, g, b):
    mu = jnp.mean(z, axis=-1, keepdims=True)
    zc = z - mu
    var = jnp.mean(zc * zc, axis=-1, keepdims=True)
    return zc * lax.rsqrt(var + LN_EPS) * g + b


def _outproj_kernel(og_ref, oa_ref, wg_ref, wa_ref, x_ref, gate_ref, lg_ref, lb_ref, o_ref):
    y = _dot(og_ref[...], wg_ref[...]) + _dot(oa_ref[...], wa_ref[...])
    z = DEEPNORM_ALPHA * x_ref[...] + (1.0 + gate_ref[...]) * y
    o_ref[...] = _layer_norm(z, lg_ref[...], lb_ref[...])


def _outproj(og, oa, wo, x, gate, ln_g, ln_b):
    s, d = x.shape
    tm = min(s, 512)
    vec = pl.BlockSpec((1, d), lambda i: (0, 0))
    once = pl.Buffered(1)
    return pl.pallas_call(
        _outproj_kernel,
        grid=(s // tm,),
        in_specs=[pl.BlockSpec((tm, GLA_WIDTH), lambda i: (i, 0)),
                  pl.BlockSpec((tm, ATT_WIDTH), lambda i: (i, 0)),
                  pl.BlockSpec((GLA_WIDTH, d), lambda i: (0, 0), pipeline_mode=once),
                  pl.BlockSpec((ATT_WIDTH, d), lambda i: (GLA_WIDTH // ATT_WIDTH, 0), pipeline_mode=once),
                  pl.BlockSpec((tm, d), lambda i: (i, 0)),
                  vec, vec, vec],
        out_specs=pl.BlockSpec((tm, d), lambda i: (i, 0)),
        out_shape=jax.ShapeDtypeStruct((s, d), F32),
        compiler_params=_params("arbitrary"),
        name="out_proj_ln",
    )(og, oa, wo, wo, x, gate, ln_g, ln_b)


def _ffn_kernel(x_ref, sc_ref, sh_ref, gate_ref, wa_ref, wg_ref, cwa_ref, cwg_ref,
                cba_ref, cbg_ref, wd_ref, lg_ref, lb_ref, o_ref,
                u_ref, ha_ref, hg_ref, ta_ref, tg_ref, *, tm, nf):
    i = pl.program_id(0)
    j = pl.program_id(1)
    H = 8
    acc_ref = o_ref

    @pl.when(j == 0)
    def _():
        u_ref[...] = (x_ref[...] * (1.0 + sc_ref[...]) + sh_ref[...]).astype(BF16)
        acc_ref[...] = jnp.zeros_like(acc_ref)

    @pl.when(i == 0)
    def _():
        ta_ref[j] = jnp.zeros(ta_ref.shape[1:], F32)
        tg_ref[j] = jnp.zeros(tg_ref.shape[1:], F32)

    ha_ref[0:H, :] = ta_ref[j]
    hg_ref[0:H, :] = tg_ref[j]

    R = min(tm, FFN_ROWS)

    def up(r):
        u = u_ref[r:r + R, :]
        ha = _dot(u, wa_ref[...])
        hg = _dot(u, wg_ref[...])
        ha_ref[H + r:H + r + R, :] = ha
        hg_ref[H + r:H + r + R, :] = hg
        return ha, hg

    def conv(h, cw_ref, cb_ref, h_ref, r):
        cw = cw_ref[...]
        return (cb_ref[...] + cw[0:1, :] * h_ref[H + r - 2:H + r - 2 + R, :]
                + cw[1:2, :] * h_ref[H + r - 1:H + r - 1 + R, :] + cw[2:3, :] * h)

    nxt = up(0)
    for r in range(0, tm, R):
        ha, hg = nxt
        if r + R < tm:
            nxt = up(r + R)
        ya = conv(ha, cwa_ref, cba_ref, ha_ref, r)
        yg = conv(hg, cwg_ref, cbg_ref, hg_ref, r)
        act = (yg / (1.0 + jnp.exp(-yg)) * ya).astype(BF16)
        acc_ref[r:r + R, :] += _dot(act, wd_ref[...])

    ta_ref[j] = ha_ref[tm:tm + H, :]
    tg_ref[j] = hg_ref[tm:tm + H, :]

    @pl.when(j == nf - 1)
    def _():
        z = DEEPNORM_ALPHA * x_ref[...] + (1.0 + gate_ref[...]) * acc_ref[...]
        o_ref[...] = _layer_norm(z, lg_ref[...], lb_ref[...])


def _ffn(x, sc, sh, gate, w_up, conv_w, conv_b, w_down, ln_g, ln_b):
    s, d = x.shape
    f = w_down.shape[0]
    tm = min(s, 1024)
    fc = 512
    nf = f // fc
    vec = pl.BlockSpec((1, d), lambda i, j: (0, 0))
    return pl.pallas_call(
        functools.partial(_ffn_kernel, tm=tm, nf=nf),
        grid=(s // tm, nf),
        in_specs=[pl.BlockSpec((tm, d), lambda i, j: (i, 0), pipeline_mode=pl.Buffered(1)),
                  vec, vec, vec,
                  pl.BlockSpec((d, fc), lambda i, j: (0, j)),
                  pl.BlockSpec((d, fc), lambda i, j: (0, j + nf)),
                  pl.BlockSpec((3, fc), lambda i, j: (0, j)),
                  pl.BlockSpec((3, fc), lambda i, j: (0, j + nf)),
                  pl.BlockSpec((1, fc), lambda i, j: (0, j)),
                  pl.BlockSpec((1, fc), lambda i, j: (0, j + nf)),
                  pl.BlockSpec((fc, d), lambda i, j: (j, 0)),
                  vec, vec],
        out_specs=pl.BlockSpec((tm, d), lambda i, j: (i, 0)),
        out_shape=jax.ShapeDtypeStruct((s, d), F32),
        scratch_shapes=[pltpu.VMEM((tm, d), BF16),
                        pltpu.VMEM((tm + 8, fc), F32),
                        pltpu.VMEM((tm + 8, fc), F32),
                        pltpu.VMEM((nf, 8, fc), F32),
                        pltpu.VMEM((nf, 8, fc), F32)],
        compiler_params=_params("arbitrary", "arbitrary"),
        name="conv_ffn_ln",
    )(x, sc, sh, gate, w_up, w_up, conv_w, conv_w, conv_b, conv_b, w_down, ln_g, ln_b)


def kernel(x, c, positions, w_ada, b_ada, w_in, w_gla_gate, b_gla_gate, gla_norm_g, w_out,
           ln1_g, ln1_b, w_up, conv_w, conv_b, w_down, ln2_g, ln2_b):
    batch, s, d = x.shape
    half = ATT_HD // 2
    inv_freq = ROPE_THETA ** (-jnp.arange(half, dtype=F32) / half)
    invf = jnp.concatenate([inv_freq, inv_freq]).reshape(1, ATT_HD)
    sign = jnp.concatenate([-jnp.ones((half,), F32), jnp.ones((half,), F32)]).reshape(1, ATT_HD)
    outs = []
    for bi in range(batch):
        xb = x.reshape(s, d) if batch == 1 else x[bi]
        for layer in range(w_in.shape[0]):
            mod = _ada(c[bi].reshape(d, 1), w_ada[layer], b_ada[layer].reshape(1, -1))
            sh1, sc1, g1, sh2, sc2, g2 = [mod[:, k * d:(k + 1) * d] for k in range(6)]
            w_main, w_lr = _wprep(jnp.transpose(w_in[layer]))
            wg_pad = jnp.pad(w_gla_gate[layer], ((0, LANES - GLA_RANK), (0, 0)))
            cos, sin, wo = _rope_tables(positions[bi].reshape(s, 1), invf, sign, w_out[layer])
            proj, glr = _inproj(xb, sc1, sh1, w_main, w_lr, cos, sin)
            og, w_up_b = _gla(proj, glr, wg_pad, b_gla_gate[layer].reshape(1, -1),
                              gla_norm_g[layer].reshape(1, -1), w_up[layer])
            oa, w_down_b = _attention(proj, w_down[layer])
            xb = _outproj(og, oa, wo, xb, g1,
                          ln1_g[layer].reshape(1, -1), ln1_b[layer].reshape(1, -1))
            xb = _ffn(xb, sc2, sh2, g2, w_up_b, conv_w[layer],
                      conv_b[layer].reshape(1, -1), w_down_b,
                      ln2_g[layer].reshape(1, -1), ln2_b[layer].reshape(1, -1))
        outs.append(xb)
    return outs[0].reshape(1, s, d) if batch == 1 else jnp.stack(outs, axis=0)
```

```python
import functools

import jax
import jax.numpy as jnp
from jax import lax
from jax.experimental import pallas as pl
from jax.experimental.pallas import tpu as pltpu

F32 = jnp.float32
BF16 = jnp.bfloat16

D_MODEL = 2048
GLA_HEADS = 4
GLA_DK = 128
GLA_DV = 256
GLA_WIDTH = GLA_HEADS * GLA_DV
GLA_QK = GLA_HEADS * GLA_DK
GLA_RANK = 16
GLA_TAU = 16.0
ATT_HEADS = 8
ATT_HD = 128
ATT_WIDTH = ATT_HEADS * ATT_HD
ATT_SPAN = 128
ATT_DILATIONS = (1, 4, 16)
D_FF = 5632
ROPE_THETA = 10000.0
DEEPNORM_ALPHA = 2.0 ** 0.25
LN_EPS = 1e-5
RMS_EPS = 1e-6
LOG2_E = 1.4426950408889634

LANES = 128
GLA_CHUNK = 128
GLA_STEP_CHUNKS = 4
GLA_LEVELS = (64, 32, 16, 8)
GLA_DIAG = 8
ATT_SUPER = ATT_SPAN * max(ATT_DILATIONS)
FFN_SPLIT = (512, 512)
NEG_BIG = -1e30
VMEM_LIMIT = 56 * 1024 * 1024

COL_AQ, COL_AK = 0, 1024
COL_GQ, COL_GK, COL_GV, COL_GR = 2048, 2560, 3072, 4096
COL_AV = 5120
PROJ_COLS = 6144
PROJ_TN = 1024
PROJ_SUB = 512


def _dot(a, b):
    return jnp.dot(a, b, preferred_element_type=F32)


def _dot_nt(a, b):
    return lax.dot_general(a, b, (((1,), (1,)), ((), ())), preferred_element_type=F32)


def _dot_tn(a, b):
    return lax.dot_general(a, b, (((0,), (0,)), ((), ())), preferred_element_type=F32)


def _split_bf16(a):
    hi = a.astype(BF16)
    lo = (a - hi.astype(F32)).astype(BF16)
    return hi, lo


def _params(*sem):
    return pltpu.CompilerParams(dimension_semantics=sem, vmem_limit_bytes=VMEM_LIMIT)


def _ada_kernel(c_ref, w_ref, b_ref, o_ref):
    cc = c_ref[...]
    s = cc / (1.0 + jnp.exp(-cc))
    o_ref[...] = jnp.sum(s * w_ref[...], axis=0, keepdims=True) + b_ref[...]


def _ada(c_col, w_ada, b_ada):
    d, n = w_ada.shape
    tn = 2048
    return pl.pallas_call(
        _ada_kernel,
        grid=(n // tn,),
        in_specs=[pl.BlockSpec((d, 1), lambda j: (0, 0)),
                  pl.BlockSpec((d, tn), lambda j: (0, j)),
                  pl.BlockSpec((1, tn), lambda j: (0, j))],
        out_specs=pl.BlockSpec((1, tn), lambda j: (0, j)),
        out_shape=jax.ShapeDtypeStruct((1, n), F32),
        compiler_params=_params("arbitrary"),
        name="ada_mod",
    )(c_col, w_ada, b_ada)


def _rope_kernel(pos_ref, invf_ref, sign_ref, wcast_ref, cos_ref, sin_ref, wcast_out_ref):
    ang = pos_ref[...].astype(F32) * invf_ref[...]
    cos_ref[...] = jnp.cos(ang)
    sin_ref[...] = jnp.sin(ang) * sign_ref[...]
    wcast_out_ref[...] = wcast_ref[...].astype(BF16)


def _cast_rider_specs(w, nsteps, step_of):
    rows, cols = w.shape
    tr = rows // nsteps
    assert tr * nsteps == rows and tr % 16 == 0, (rows, nsteps)
    spec = pl.BlockSpec((tr, cols), lambda *idx: (step_of(*idx), 0))
    return spec, spec, jax.ShapeDtypeStruct((rows, cols), BF16)


def _rope_tables(pos_col, invf, sign, w_cast):
    s = pos_col.shape[0]
    tm = min(s, 1024)
    cast_in, cast_out, cast_shape = _cast_rider_specs(w_cast, s // tm, lambda i: i)
    return pl.pallas_call(
        _rope_kernel,
        grid=(s // tm,),
        in_specs=[pl.BlockSpec((tm, 1), lambda i: (i, 0)),
                  pl.BlockSpec((1, LANES), lambda i: (0, 0)),
                  pl.BlockSpec((1, LANES), lambda i: (0, 0)),
                  cast_in],
        out_specs=[pl.BlockSpec((tm, LANES), lambda i: (i, 0)),
                   pl.BlockSpec((tm, LANES), lambda i: (i, 0)),
                   cast_out],
        out_shape=[jax.ShapeDtypeStruct((s, LANES), F32)] * 2 + [cast_shape],
        compiler_params=_params("arbitrary"),
        name="rope_tables",
    )(pos_col, invf, sign, w_cast)


SRC_LR = 2 * GLA_QK + 2 * GLA_WIDTH
SRC_AQ = SRC_LR + GLA_RANK
SRC_AV = SRC_AQ + 2 * ATT_WIDTH


def _wprep_kernel(w_ref, lrsrc_ref, main_ref, lr_ref):
    main_ref[...] = w_ref[...].astype(BF16)

    @pl.when(pl.program_id(0) == 0)
    def _():
        lr_ref[0:GLA_RANK, :] = lrsrc_ref[...].astype(BF16)
        lr_ref[GLA_RANK:, :] = jnp.zeros((LANES - GLA_RANK, lr_ref.shape[1]), BF16)


def _wprep(w_t):
    n, d = w_t.shape
    tr = 1024

    def src_row(i):
        r = i * (tr // 8)
        tile = jnp.where(r < COL_GQ // 8, SRC_AQ // 8 + r,
                         jnp.where(r < COL_AV // 8, r - COL_GQ // 8, (SRC_AV - COL_AV) // 8 + r))
        return tile * 8

    return pl.pallas_call(
        _wprep_kernel,
        grid=(PROJ_COLS // tr,),
        in_specs=[pl.BlockSpec((pl.Element(tr), pl.Element(d)), lambda i: (src_row(i), 0)),
                  pl.BlockSpec((pl.Element(GLA_RANK), pl.Element(d)), lambda i: (SRC_LR, 0))],
        out_specs=[pl.BlockSpec((tr, d), lambda i: (i, 0)),
                   pl.BlockSpec((LANES, d), lambda i: (0, 0))],
        out_shape=[jax.ShapeDtypeStruct((PROJ_COLS, d), BF16),
                   jax.ShapeDtypeStruct((LANES, d), BF16)],
        compiler_params=_params("arbitrary"),
        name="w_in_prep",
    )(w_t, w_t)


def _inproj_kernel(x_ref, sc_ref, sh_ref, w_ref, wlr_ref, cos_ref, sin_ref,
                   proj_ref, glr_ref, u_ref, *, tn):
    j = pl.program_id(1)

    @pl.when(j == 0)
    def _():
        u = (x_ref[...] * (1.0 + sc_ref[...]) + sh_ref[...]).astype(BF16)
        u_ref[...] = u
        glr_ref[...] = _dot_nt(u, wlr_ref[...])

    col = j * tn
    q_scale = ATT_HD ** -0.5

    def sub_dot(n):
        return _dot_nt(u_ref[...], w_ref[n:n + PROJ_SUB, :])

    @pl.when(col < COL_GQ)
    def _():
        cos = cos_ref[...]
        sin = sin_ref[...]
        for n in range(0, tn, PROJ_SUB):
            acc = sub_dot(n) * jnp.where(col + n < COL_AK, q_scale, 1.0).astype(F32)
            for h in range(n, n + PROJ_SUB, ATT_HD):
                t = acc[:, h - n:h - n + ATT_HD]
                proj_ref[:, h:h + ATT_HD] = t * cos + pltpu.roll(t, ATT_HD // 2, axis=1) * sin

    @pl.when(col >= COL_GQ)
    def _():
        for n in range(0, tn, PROJ_SUB):
            acc = sub_dot(n)
            if n < GLA_QK:
                acc = acc * jnp.where(col == COL_GQ, q_scale, 1.0).astype(F32)
            proj_ref[:, n:n + PROJ_SUB] = acc


def _inproj(x, sc, sh, w_main, w_lr, cos, sin):
    s, d = x.shape
    tm = min(s, 1024)
    tn = PROJ_TN
    return pl.pallas_call(
        functools.partial(_inproj_kernel, tn=tn),
        grid=(s // tm, PROJ_COLS // tn),
        in_specs=[pl.BlockSpec((tm, d), lambda i, j: (i, 0)),
                  pl.BlockSpec((1, d), lambda i, j: (0, 0)),
                  pl.BlockSpec((1, d), lambda i, j: (0, 0)),
                  pl.BlockSpec((tn, d), lambda i, j: (j, 0)),
                  pl.BlockSpec((LANES, d), lambda i, j: (0, 0)),
                  pl.BlockSpec((tm, LANES), lambda i, j: (i, 0)),
                  pl.BlockSpec((tm, LANES), lambda i, j: (i, 0))],
        out_specs=[pl.BlockSpec((tm, tn), lambda i, j: (i, j)),
                   pl.BlockSpec((tm, LANES), lambda i, j: (i, 0))],
        out_shape=[jax.ShapeDtypeStruct((s, PROJ_COLS), F32),
                   jax.ShapeDtypeStruct((s, LANES), F32)],
        scratch_shapes=[pltpu.VMEM((tm, d), BF16)],
        compiler_params=_params("arbitrary", "arbitrary"),
        name="in_proj",
    )(x, sc, sh, w_main, w_lr, cos, sin)


def _gla_kernel(q_ref, k_ref, v_ref, gr_ref, glr_ref, wgh_ref, wgl_ref, bg_ref, gn_ref, wcast_ref,
                o_ref, wcast_out_ref, state_ref):
    c = pl.program_id(0)
    C = GLA_CHUNK
    wcast_out_ref[...] = wcast_ref[...].astype(BF16)

    @pl.when(c == 0)
    def _():
        state_ref[...] = jnp.zeros_like(state_ref)

    row = lax.broadcasted_iota(jnp.int32, (C, C), 0)
    col = lax.broadcasted_iota(jnp.int32, (C, C), 1)

    tri = (col <= row).astype(BF16)

    nlev = len(GLA_LEVELS)
    r4 = lax.broadcasted_iota(jnp.int32, (nlev * C, C), 0)
    c4 = lax.broadcasted_iota(jnp.int32, (nlev * C, C), 1)
    sel = jnp.zeros((nlev * C, C), F32)
    for li, m in enumerate(GLA_LEVELS):
        t = r4 - li * C
        hit = (t >= 0) & (t < C) & (c4 == (t // (2 * m)) * (2 * m) + m - 1)
        sel = jnp.where(hit, 1.0, sel)
    sel = sel.astype(BF16)

    g3 = lax.broadcasted_iota(jnp.int32, (C // GLA_DIAG, GLA_DIAG, C), 0)
    t3 = lax.broadcasted_iota(jnp.int32, (C // GLA_DIAG, GLA_DIAG, C), 1)
    s3 = lax.broadcasted_iota(jnp.int32, (C // GLA_DIAG, GLA_DIAG, C), 2)

    for rows in [slice(r, r + C) for r in range(0, q_ref.shape[0], C)]:
        _gla_chunk(rows, q_ref, k_ref, v_ref, gr_ref, glr_ref, wgh_ref, wgl_ref, bg_ref, gn_ref,
                   o_ref, state_ref, row, col, tri, sel, g3, t3, s3)


def _gla_chunk(rows, q_ref, k_ref, v_ref, gr_ref, glr_ref, wgh_ref, wgl_ref, bg_ref, gn_ref,
               o_ref, state_ref, row, col, tri, sel, g3, t3, s3):
    C = GLA_CHUNK
    g_hi, g_lo = _split_bf16(glr_ref[rows, :])
    w_hi, w_lo = wgh_ref[...], wgl_ref[...]
    z = _dot(g_hi, w_hi) + _dot(g_hi, w_lo) + _dot(g_lo, w_hi) + bg_ref[...]
    log_a = (jnp.minimum(z, 0.0) - jnp.log1p(jnp.exp(-jnp.abs(z)))) * (LOG2_E / GLA_TAU)

    a_hi, a_lo = _split_bf16(log_a)
    bcum = _dot(tri, a_hi) + _dot(tri, a_lo)

    ref_all = _dot(sel, bcum.astype(BF16))

    for h in range(GLA_HEADS):
        ks = slice(h * GLA_DK, (h + 1) * GLA_DK)
        vs = slice(h * GLA_DV, (h + 1) * GLA_DV)
        q = q_ref[rows, ks]
        k = k_ref[rows, ks]
        vb = v_ref[rows, vs].astype(BF16)
        b = bcum[:, ks]
        b_last = b[C - 1:C, :]
        state = state_ref[h]

        o = _dot((q * jnp.exp2(b)).astype(BF16), state.astype(BF16))

        scores = jnp.zeros((C, C), F32)
        for li, m in enumerate(GLA_LEVELS):
            ref = ref_all[li * C:(li + 1) * C, ks]
            upper = (row & m) != 0
            qm = q * jnp.exp2(jnp.where(upper, b - ref, NEG_BIG))
            km = k * jnp.exp2(jnp.where(upper, NEG_BIG, ref - b))
            part = _dot_nt(qm.astype(BF16), km.astype(BF16))
            same = (row // (2 * m)) == (col // (2 * m))
            scores = scores + jnp.where(same, part, 0.0)

        groups = C // GLA_DIAG
        q3 = q.reshape(groups, GLA_DIAG, GLA_DK)
        k3 = k.reshape(groups, GLA_DIAG, GLA_DK)
        b3 = b.reshape(groups, GLA_DIAG, GLA_DK)
        diag = jnp.zeros((groups, GLA_DIAG, C), F32)
        for jj in range(GLA_DIAG):
            term = q3 * k3[:, jj:jj + 1, :] * jnp.exp2(b3 - b3[:, jj:jj + 1, :])
            colsum = jnp.sum(term, axis=-1, keepdims=True)
            hit = (s3 == g3 * GLA_DIAG + jj) & (t3 >= jj)
            diag = jnp.where(hit, colsum, diag)
        scores = scores + diag.reshape(C, C)

        o = o + _dot(scores.astype(BF16), vb)

        kl = (k * jnp.exp2(b_last - b)).astype(BF16)
        decay_col = jnp.transpose(jnp.broadcast_to(jnp.exp2(b_last), (C, GLA_DK)))
        decay = jnp.concatenate([decay_col, decay_col], axis=1)
        state_ref[h] = state * decay + _dot_tn(kl, vb)

        rs = lax.rsqrt(jnp.mean(o * o, axis=-1, keepdims=True) + RMS_EPS)
        gate = gr_ref[rows, vs]
        gate = gate / (1.0 + jnp.exp(-gate))
        o_ref[rows, vs] = (o * rs * gn_ref[...] * gate).astype(o_ref.dtype)


def _gla(proj, glr, wg_pad, bg, gnorm, w_cast):
    s = proj.shape[0]
    C = GLA_CHUNK * GLA_STEP_CHUNKS
    wg_hi, wg_lo = _split_bf16(wg_pad)
    cast_in, cast_out, cast_shape = _cast_rider_specs(w_cast, s // C, lambda c: c)
    return pl.pallas_call(
        _gla_kernel,
        grid=(s // C,),
        in_specs=[pl.BlockSpec((C, GLA_QK), lambda c: (c, COL_GQ // GLA_QK)),
                  pl.BlockSpec((C, GLA_QK), lambda c: (c, COL_GK // GLA_QK)),
                  pl.BlockSpec((C, GLA_WIDTH), lambda c: (c, COL_GV // GLA_WIDTH)),
                  pl.BlockSpec((C, GLA_WIDTH), lambda c: (c, COL_GR // GLA_WIDTH)),
                  pl.BlockSpec((C, LANES), lambda c: (c, 0)),
                  pl.BlockSpec((LANES, GLA_QK), lambda c: (0, 0)),
                  pl.BlockSpec((LANES, GLA_QK), lambda c: (0, 0)),
                  pl.BlockSpec((1, GLA_QK), lambda c: (0, 0)),
                  pl.BlockSpec((1, GLA_DV), lambda c: (0, 0)),
                  cast_in],
        out_specs=[pl.BlockSpec((C, GLA_WIDTH), lambda c: (c, 0)), cast_out],
        out_shape=[jax.ShapeDtypeStruct((s, GLA_WIDTH), BF16), cast_shape],
        scratch_shapes=[pltpu.VMEM((GLA_HEADS, GLA_DK, GLA_DV), F32)],
        compiler_params=_params("arbitrary"),
        name="gla",
    )(proj, proj, proj, proj, glr, wg_hi, wg_lo, bg, gnorm, w_cast)


def _att_kernel(q_ref, k_ref, v_ref, wcast_ref, o_ref, wcast_out_ref,
                q1, q4, q16, k1, k4, k16, v1, v4, v16,
                qs_ref, ks_ref, vs_ref, bias_ref, acc_ref, lse_ref):
    n = pl.program_id(1)
    wcast_out_ref[...] = wcast_ref[...].astype(BF16)
    B = ATT_SPAN
    T = ATT_SUPER
    qd, kd, vd = (q1, q4, q16), (k1, k4, k16), (v1, v4, v16)

    row = lax.broadcasted_iota(jnp.int32, (B, 2 * B), 0)
    col = lax.broadcasted_iota(jnp.int32, (B, 2 * B), 1)
    band = (col >= row) & (col <= row + B)
    bias_ref[0] = jnp.where(band, 0.0, NEG_BIG)
    bias_ref[1] = jnp.where(band & (col >= B), 0.0, NEG_BIG)

    @pl.when(n == 0)
    def _():
        for d, kb, vb in zip(ATT_DILATIONS, kd, vd):
            pitch = T // d + B
            for r in range(d):
                kb[r * pitch:r * pitch + B, :] = jnp.zeros((B, ATT_HD), BF16)
                vb[r * pitch:r * pitch + B, :] = jnp.zeros((B, ATT_HD), BF16)

    @pl.when(n > 0)
    def _():
        for d, kb, vb in zip(ATT_DILATIONS, kd, vd):
            pitch = T // d + B
            for r in range(d):
                kb[r * pitch:r * pitch + B, :] = kb[(r + 1) * pitch - B:(r + 1) * pitch, :]
                vb[r * pitch:r * pitch + B, :] = vb[(r + 1) * pitch - B:(r + 1) * pitch, :]

    assert ATT_DILATIONS == (1, 4, 16)
    for src, stage, bufs, front in ((q_ref, qs_ref, qd, 0), (k_ref, ks_ref, kd, B), (v_ref, vs_ref, vd, B)):
        b1, b4, b16 = bufs
        b1[front:front + T, :] = src[...].astype(BF16)
        L4, L16 = T // 4, T // 16
        for r in range(4):
            val = src[pl.ds(r, L4, stride=4), :]
            stage[r * L4:(r + 1) * L4, :] = val
            lo = r * (L4 + front) + front
            b4[lo:lo + L4, :] = val.astype(BF16)
        for r in range(16):
            val = stage[pl.ds((r % 4) * L4 + r // 4, L16, stride=4), :]
            lo = r * (L16 + front) + front
            b16[lo:lo + L16, :] = val.astype(BF16)

    ones = jnp.ones((2 * B, ATT_HD), BF16)

    for bi, (d, qb, kb, vb) in enumerate(zip(ATT_DILATIONS, qd, kd, vd)):
        nblk = T // (B * d)
        pitch = T // d + B

        def body(it, carry, bi=bi, d=d, qb=qb, kb=kb, vb=vb, nblk=nblk, pitch=pitch):
            r = it // nblk
            blk = it % nblk
            qs = qb[pl.ds(pl.multiple_of(it * B, B), B), :]
            kstart = pl.multiple_of(r * pitch + blk * B, B)
            kcat = kb[pl.ds(kstart, 2 * B), :]
            vcat = vb[pl.ds(kstart, 2 * B), :]
            no_prev = jnp.where((n == 0) & (blk == 0), 1, 0)
            s = _dot_nt(qs, kcat) + bias_ref[no_prev]
            m = jnp.max(s, axis=1, keepdims=True)
            p = jnp.exp(s - m).astype(BF16)
            pv = _dot(p, jnp.concatenate([vcat, ones], axis=1))
            start = r + blk * B * d
            dst = pl.ds(start, B, stride=d) if d > 1 else pl.ds(pl.multiple_of(start, B), B)
            l = pv[:, ATT_HD:]
            acc_ref[bi, dst, :] = pv[:, :ATT_HD] / l
            lse_ref[bi, dst, :] = m + jnp.log(l)
            return carry

        lax.fori_loop(0, T // B, body, 0, unroll=16)

    s0, s1, s2 = lse_ref[0], lse_ref[1], lse_ref[2]
    mm = jnp.maximum(jnp.maximum(s0, s1), s2)
    c0, c1, c2 = jnp.exp(s0 - mm), jnp.exp(s1 - mm), jnp.exp(s2 - mm)
    num = c0 * acc_ref[0] + c1 * acc_ref[1] + c2 * acc_ref[2]
    o_ref[...] = (num / (c0 + c1 + c2)).astype(o_ref.dtype)


def _attention(proj, w_cast):
    s = proj.shape[0]
    T = ATT_SUPER
    ns = s // T
    qc, kc, vc = COL_AQ // ATT_HD, COL_AK // ATT_HD, COL_AV // ATT_HD
    blk = (T, ATT_HD)
    nb = len(ATT_DILATIONS)
    q_bufs = [pltpu.VMEM((T, ATT_HD), BF16) for _ in ATT_DILATIONS]
    kv_bufs = [pltpu.VMEM((T + d * ATT_SPAN, ATT_HD), BF16) for d in ATT_DILATIONS]
    cast_in, cast_out, cast_shape = _cast_rider_specs(w_cast, ATT_HEADS * ns, lambda h, n: h * ns + n)
    return pl.pallas_call(
        _att_kernel,
        grid=(ATT_HEADS, ns),
        in_specs=[pl.BlockSpec(blk, lambda h, n: (n, qc + h)),
                  pl.BlockSpec(blk, lambda h, n: (n, kc + h)),
                  pl.BlockSpec(blk, lambda h, n: (n, vc + h)),
                  cast_in],
        out_specs=[pl.BlockSpec(blk, lambda h, n: (n, h)), cast_out],
        out_shape=[jax.ShapeDtypeStruct((s, ATT_WIDTH), BF16), cast_shape],
        scratch_shapes=q_bufs + kv_bufs + kv_bufs
        + [pltpu.VMEM((T, ATT_HD), F32)] * 3
        + [pltpu.VMEM((2, ATT_SPAN, 2 * ATT_SPAN), F32)]
        + [pltpu.VMEM((nb, T, LANES), F32)] * 2,
        compiler_params=_params("arbitrary", "arbitrary"),
        name="dilated_attn",
    )(proj, proj, proj, w_cast)


def _layer_norm(z, g, b):
    mu = jnp.mean(z, axis=-1, keepdims=True)
    zc = z - mu
    var = jnp.mean(zc * zc, axis=-1, keepdims=True)
    return zc * lax.rsqrt(var + LN_EPS) * g + b


def _outproj_kernel(og_ref, oa_ref, wg_ref, wa_ref, x_ref, gate_ref, lg_ref, lb_ref, o_ref):
    y = _dot(og_ref[...], wg_ref[...]) + _dot(oa_ref[...], wa_ref[...])
    z = DEEPNORM_ALPHA * x_ref[...] + (1.0 + gate_ref[...]) * y
    o_ref[...] = _layer_norm(z, lg_ref[...], lb_ref[...])


def _outproj(og, oa, wo, x, gate, ln_g, ln_b):
    s, d = x.shape
    tm = min(s, 512)
    vec = pl.BlockSpec((1, d), lambda i: (0, 0))
    return pl.pallas_call(
        _outproj_kernel,
        grid=(s // tm,),
        in_specs=[pl.BlockSpec((tm, GLA_WIDTH), lambda i: (i, 0)),
                  pl.BlockSpec((tm, ATT_WIDTH), lambda i: (i, 0)),
                  pl.BlockSpec((GLA_WIDTH, d), lambda i: (0, 0)),
                  pl.BlockSpec((ATT_WIDTH, d), lambda i: (GLA_WIDTH // ATT_WIDTH, 0)),
                  pl.BlockSpec((tm, d), lambda i: (i, 0)),
                  vec, vec, vec],
        out_specs=pl.BlockSpec((tm, d), lambda i: (i, 0)),
        out_shape=jax.ShapeDtypeStruct((s, d), F32),
        compiler_params=_params("arbitrary"),
        name="out_proj_ln",
    )(og, oa, wo, wo, x, gate, ln_g, ln_b)


def _ffn_kernel(x_ref, sc_ref, sh_ref, gate_ref, wa_ref, wg_ref, cwa_ref, cwg_ref,
                cba_ref, cbg_ref, wd_ref, lg_ref, lb_ref, o_ref,
                u_ref, ta_ref, tg_ref, *, tm, nf):
    i = pl.program_id(0)
    j = pl.program_id(1)
    H = 8
    acc_ref = o_ref

    @pl.when(j == 0)
    def _():
        u_ref[...] = (x_ref[...] * (1.0 + sc_ref[...]) + sh_ref[...]).astype(BF16)
        acc_ref[...] = jnp.zeros_like(acc_ref)

    @pl.when(i == 0)
    def _():
        ta_ref[j] = jnp.zeros(ta_ref.shape[1:], F32)
        tg_ref[j] = jnp.zeros(tg_ref.shape[1:], F32)

    sizes = list(FFN_SPLIT) if tm == sum(FFN_SPLIT) else [tm]
    blocks = [(sum(sizes[:n]), sizes[n]) for n in range(len(sizes))]

    def up(r, R):
        u = u_ref[r:r + R, :]
        return _dot(u, wa_ref[...]), _dot(u, wg_ref[...])

    def conv(h, halo, cw_ref, cb_ref):
        ext = jnp.concatenate([halo, h], axis=0)
        h1 = pltpu.roll(ext, 1, axis=0)[H:, :]
        h2 = pltpu.roll(ext, 2, axis=0)[H:, :]
        cw = cw_ref[...]
        return cb_ref[...] + cw[0:1, :] * h2 + cw[1:2, :] * h1 + cw[2:3, :] * h

    halo_a, halo_g = ta_ref[j], tg_ref[j]
    nxt = up(*blocks[0])
    for n, (r, R) in enumerate(blocks):
        ha, hg = nxt
        if n + 1 < len(blocks):
            nxt = up(*blocks[n + 1])
        ya = conv(ha, halo_a, cwa_ref, cba_ref)
        yg = conv(hg, halo_g, cwg_ref, cbg_ref)
        halo_a, halo_g = ha[R - H:, :], hg[R - H:, :]
        act = (yg / (1.0 + jnp.exp(-yg)) * ya).astype(BF16)
        acc_ref[r:r + R, :] += _dot(act, wd_ref[...])

    ta_ref[j] = halo_a
    tg_ref[j] = halo_g

    @pl.when(j == nf - 1)
    def _():
        z = DEEPNORM_ALPHA * x_ref[...] + (1.0 + gate_ref[...]) * acc_ref[...]
        o_ref[...] = _layer_norm(z, lg_ref[...], lb_ref[...])


def _ffn(x, sc, sh, gate, w_up, conv_w, conv_b, w_down, ln_g, ln_b):
    s, d = x.shape
    f = w_down.shape[0]
    tm = min(s, 1024)
    fc = 512
    nf = f // fc
    vec = pl.BlockSpec((1, d), lambda i, j: (0, 0))
    return pl.pallas_call(
        functools.partial(_ffn_kernel, tm=tm, nf=nf),
        grid=(s // tm, nf),
        in_specs=[pl.BlockSpec((tm, d), lambda i, j: (i, 0), pipeline_mode=pl.Buffered(1)),
                  vec, vec, vec,
                  pl.BlockSpec((d, fc), lambda i, j: (0, j)),
                  pl.BlockSpec((d, fc), lambda i, j: (0, j + nf)),
                  pl.BlockSpec((3, fc), lambda i, j: (0, j)),
                  pl.BlockSpec((3, fc), lambda i, j: (0, j + nf)),
                  pl.BlockSpec((1, fc), lambda i, j: (0, j)),
                  pl.BlockSpec((1, fc), lambda i, j: (0, j + nf)),
                  pl.BlockSpec((fc, d), lambda i, j: (j, 0)),
                  vec, vec],
        out_specs=pl.BlockSpec((tm, d), lambda i, j: (i, 0)),
        out_shape=jax.ShapeDtypeStruct((s, d), F32),
        scratch_shapes=[pltpu.VMEM((tm, d), BF16),
                        pltpu.VMEM((nf, 8, fc), F32),
                        pltpu.VMEM((nf, 8, fc), F32)],
        compiler_params=_params("arbitrary", "arbitrary"),
        name="conv_ffn_ln",
    )(x, sc, sh, gate, w_up, w_up, conv_w, conv_w, conv_b, conv_b, w_down, ln_g, ln_b)


def kernel(x, c, positions, w_ada, b_ada, w_in, w_gla_gate, b_gla_gate, gla_norm_g, w_out,
           ln1_g, ln1_b, w_up, conv_w, conv_b, w_down, ln2_g, ln2_b):
    batch, s, d = x.shape
    half = ATT_HD // 2
    inv_freq = ROPE_THETA ** (-jnp.arange(half, dtype=F32) / half)
    invf = jnp.concatenate([inv_freq, inv_freq]).reshape(1, ATT_HD)
    sign = jnp.concatenate([-jnp.ones((half,), F32), jnp.ones((half,), F32)]).reshape(1, ATT_HD)
    outs = []
    for bi in range(batch):
        xb = x.reshape(s, d) if batch == 1 else x[bi]
        for layer in range(w_in.shape[0]):
            mod = _ada(c[bi].reshape(d, 1), w_ada[layer], b_ada[layer].reshape(1, -1))
            sh1, sc1, g1, sh2, sc2, g2 = [mod[:, k * d:(k + 1) * d] for k in range(6)]
            w_main, w_lr = _wprep(jnp.transpose(w_in[layer]))
            wg_pad = jnp.pad(w_gla_gate[layer], ((0, LANES - GLA_RANK), (0, 0)))
            cos, sin, wo = _rope_tables(positions[bi].reshape(s, 1), invf, sign, w_out[layer])
            proj, glr = _inproj(xb, sc1, sh1, w_main, w_lr, cos, sin)
            og, w_up_b = _gla(proj, glr, wg_pad, b_gla_gate[layer].reshape(1, -1),
                              gla_norm_g[layer].reshape(1, -1), w_up[layer])
            oa, w_down_b = _attention(proj, w_down[layer])
            xb = _outproj(og, oa, wo, xb, g1,
                          ln1_g[layer].reshape(1, -1), ln1_b[layer].reshape(1, -1))
            xb = _ffn(xb, sc2, sh2, g2, w_up_b, conv_w[layer],
                      conv_b[layer].reshape(1, -1), w_down_b,
                      ln2_g[layer].reshape(1, -1), ln2_b[layer].reshape(1, -1))
        outs.append(xb)
    return outs[0].reshape(1, s, d) if batch == 1 else jnp.stack(outs, axis=0)
```

```python
import functools

import jax
import jax.numpy as jnp
from jax import lax
from jax.experimental import pallas as pl
from jax.experimental.pallas import tpu as pltpu

F32 = jnp.float32
BF16 = jnp.bfloat16

D_MODEL = 2048
GLA_HEADS = 4
GLA_DK = 128
GLA_DV = 256
GLA_WIDTH = GLA_HEADS * GLA_DV
GLA_QK = GLA_HEADS * GLA_DK
GLA_RANK = 16
GLA_TAU = 16.0
ATT_HEADS = 8
ATT_HD = 128
ATT_WIDTH = ATT_HEADS * ATT_HD
ATT_SPAN = 128
ATT_DILATIONS = (1, 4, 16)
D_FF = 5632
ROPE_THETA = 10000.0
DEEPNORM_ALPHA = 2.0 ** 0.25
LN_EPS = 1e-5
RMS_EPS = 1e-6
LOG2_E = 1.4426950408889634

LANES = 128
GLA_CHUNK = 128
GLA_STEP_CHUNKS = 4
GLA_LEVELS = (64, 32, 16, 8)
GLA_DIAG = 8
ATT_SUPER = ATT_SPAN * max(ATT_DILATIONS)
FFN_SPLIT = (512, 512)
NEG_BIG = -1e30
VMEM_LIMIT = 56 * 1024 * 1024

COL_AQ, COL_AK = 0, 1024
COL_GQ, COL_GK, COL_GV, COL_GR = 2048, 2560, 3072, 4096
COL_AV = 5120
PROJ_COLS = 6144
PROJ_TN = 1024
PROJ_SUB = 512


def _dot(a, b):
    return jnp.dot(a, b, preferred_element_type=F32)


def _dot_nt(a, b):
    return lax.dot_general(a, b, (((1,), (1,)), ((), ())), preferred_element_type=F32)


def _dot_tn(a, b):
    return lax.dot_general(a, b, (((0,), (0,)), ((), ())), preferred_element_type=F32)


def _split_bf16(a):
    hi = a.astype(BF16)
    lo = (a - hi.astype(F32)).astype(BF16)
    return hi, lo


def _params(*sem):
    return pltpu.CompilerParams(dimension_semantics=sem, vmem_limit_bytes=VMEM_LIMIT)


def _ada_kernel(c_ref, w_ref, b_ref, o_ref):
    cc = c_ref[...]
    s = cc / (1.0 + jnp.exp(-cc))
    o_ref[...] = jnp.sum(s * w_ref[...], axis=0, keepdims=True) + b_ref[...]


def _ada(c_col, w_ada, b_ada):
    d, n = w_ada.shape
    tn = 1024
    return pl.pallas_call(
        _ada_kernel,
        grid=(n // tn,),
        in_specs=[pl.BlockSpec((d, 1), lambda j: (0, 0)),
                  pl.BlockSpec((d, tn), lambda j: (0, j)),
                  pl.BlockSpec((1, tn), lambda j: (0, j))],
        out_specs=pl.BlockSpec((1, tn), lambda j: (0, j)),
        out_shape=jax.ShapeDtypeStruct((1, n), F32),
        compiler_params=_params("arbitrary"),
        name="ada_mod",
    )(c_col, w_ada, b_ada)


def _rope_kernel(pos_ref, invf_ref, wcast_ref, cos_ref, sin_ref, wcast_out_ref):
    ang = pos_ref[...].astype(F32) * invf_ref[...]
    low = lax.broadcasted_iota(jnp.int32, ang.shape, 1) < ATT_HD // 2
    sin = jnp.sin(ang)
    cos_ref[...] = jnp.cos(ang)
    sin_ref[...] = jnp.where(low, -sin, sin)
    wcast_out_ref[...] = wcast_ref[...].astype(BF16)


def _cast_rider_specs(w, nsteps, step_of):
    rows, cols = w.shape
    tr = rows // nsteps
    assert tr * nsteps == rows and tr % 16 == 0, (rows, nsteps)
    spec = pl.BlockSpec((tr, cols), lambda *idx: (step_of(*idx), 0))
    return spec, spec, jax.ShapeDtypeStruct((rows, cols), BF16)


def _rope_tables(pos_col, invf, w_cast):
    s = pos_col.shape[0]
    tm = min(s, 1024)
    cast_in, cast_out, cast_shape = _cast_rider_specs(w_cast, s // tm, lambda i: i)
    return pl.pallas_call(
        _rope_kernel,
        grid=(s // tm,),
        in_specs=[pl.BlockSpec((tm, 1), lambda i: (i, 0)),
                  pl.BlockSpec((1, LANES), lambda i: (0, 0)),
                  cast_in],
        out_specs=[pl.BlockSpec((tm, LANES), lambda i: (i, 0)),
                   pl.BlockSpec((tm, LANES), lambda i: (i, 0)),
                   cast_out],
        out_shape=[jax.ShapeDtypeStruct((s, LANES), F32)] * 2 + [cast_shape],
        compiler_params=_params("arbitrary"),
        name="rope_tables",
    )(pos_col, invf, w_cast)


SRC_LR = 2 * GLA_QK + 2 * GLA_WIDTH
SRC_AQ = SRC_LR + GLA_RANK
SRC_AV = SRC_AQ + 2 * ATT_WIDTH


def _wprep_kernel(w_ref, lrsrc_ref, main_ref, lr_ref):
    main_ref[...] = w_ref[...].astype(BF16)

    @pl.when(pl.program_id(0) == 0)
    def _():
        lr_ref[0:GLA_RANK, :] = lrsrc_ref[...].astype(BF16)
        lr_ref[GLA_RANK:, :] = jnp.zeros((LANES - GLA_RANK, lr_ref.shape[1]), BF16)


def _wprep(w_t):
    n, d = w_t.shape
    tr = 1024

    def src_row(i):
        r = i * (tr // 8)
        tile = jnp.where(r < COL_GQ // 8, SRC_AQ // 8 + r,
                         jnp.where(r < COL_AV // 8, r - COL_GQ // 8, (SRC_AV - COL_AV) // 8 + r))
        return tile * 8

    return pl.pallas_call(
        _wprep_kernel,
        grid=(PROJ_COLS // tr,),
        in_specs=[pl.BlockSpec((pl.Element(tr), pl.Element(d)), lambda i: (src_row(i), 0)),
                  pl.BlockSpec((pl.Element(GLA_RANK), pl.Element(d)), lambda i: (SRC_LR, 0))],
        out_specs=[pl.BlockSpec((tr, d), lambda i: (i, 0)),
                   pl.BlockSpec((LANES, d), lambda i: (0, 0))],
        out_shape=[jax.ShapeDtypeStruct((PROJ_COLS, d), BF16),
                   jax.ShapeDtypeStruct((LANES, d), BF16)],
        compiler_params=_params("arbitrary"),
        name="w_in_prep",
    )(w_t, w_t)


def _inproj_kernel(x_ref, sc_ref, sh_ref, w_ref, wlr_ref, cos_ref, sin_ref,
                   proj_ref, glr_ref, u_ref, *, tn):
    j = pl.program_id(1)

    @pl.when(j == 0)
    def _():
        u = (x_ref[...] * (1.0 + sc_ref[...]) + sh_ref[...]).astype(BF16)
        u_ref[...] = u
        glr_ref[...] = _dot_nt(u, wlr_ref[...])

    col = j * tn
    q_scale = ATT_HD ** -0.5

    def sub_dot(n):
        return _dot_nt(u_ref[...], w_ref[n:n + PROJ_SUB, :])

    @pl.when(col < COL_GQ)
    def _():
        cos = cos_ref[...]
        sin = sin_ref[...]
        for n in range(0, tn, PROJ_SUB):
            acc = sub_dot(n) * jnp.where(col + n < COL_AK, q_scale * LOG2_E, 1.0).astype(F32)
            for h in range(n, n + PROJ_SUB, ATT_HD):
                t = acc[:, h - n:h - n + ATT_HD]
                proj_ref[:, h:h + ATT_HD] = t * cos + pltpu.roll(t, ATT_HD // 2, axis=1) * sin

    @pl.when(col >= COL_GQ)
    def _():
        for n in range(0, tn, PROJ_SUB):
            acc = sub_dot(n)
            if n < GLA_QK:
                acc = acc * jnp.where(col == COL_GQ, q_scale, 1.0).astype(F32)
            proj_ref[:, n:n + PROJ_SUB] = acc


def _inproj(x, sc, sh, w_main, w_lr, cos, sin):
    s, d = x.shape
    tm = min(s, 1024)
    tn = PROJ_TN
    return pl.pallas_call(
        functools.partial(_inproj_kernel, tn=tn),
        grid=(s // tm, PROJ_COLS // tn),
        in_specs=[pl.BlockSpec((tm, d), lambda i, j: (i, 0)),
                  pl.BlockSpec((1, d), lambda i, j: (0, 0)),
                  pl.BlockSpec((1, d), lambda i, j: (0, 0)),
                  pl.BlockSpec((tn, d), lambda i, j: (j, 0)),
                  pl.BlockSpec((LANES, d), lambda i, j: (0, 0)),
                  pl.BlockSpec((tm, LANES), lambda i, j: (i, 0)),
                  pl.BlockSpec((tm, LANES), lambda i, j: (i, 0))],
        out_specs=[pl.BlockSpec((tm, tn), lambda i, j: (i, j)),
                   pl.BlockSpec((tm, LANES), lambda i, j: (i, 0))],
        out_shape=[jax.ShapeDtypeStruct((s, PROJ_COLS), F32),
                   jax.ShapeDtypeStruct((s, LANES), F32)],
        scratch_shapes=[pltpu.VMEM((tm, d), BF16)],
        compiler_params=_params("arbitrary", "arbitrary"),
        name="in_proj",
    )(x, sc, sh, w_main, w_lr, cos, sin)


def _gla_kernel(q_ref, k_ref, v_ref, gr_ref, glr_ref, wgh_ref, wgl_ref, bg_ref, gn_ref, wcast_ref,
                o_ref, wcast_out_ref, state_ref):
    c = pl.program_id(0)
    C = GLA_CHUNK
    wcast_out_ref[...] = wcast_ref[...].astype(BF16)

    @pl.when(c == 0)
    def _():
        state_ref[...] = jnp.zeros_like(state_ref)

    row = lax.broadcasted_iota(jnp.int32, (C, C), 0)
    col = lax.broadcasted_iota(jnp.int32, (C, C), 1)

    tri = (col <= row).astype(BF16)

    nlev = len(GLA_LEVELS)
    r4 = lax.broadcasted_iota(jnp.int32, (nlev * C, C), 0)
    c4 = lax.broadcasted_iota(jnp.int32, (nlev * C, C), 1)
    sel = jnp.zeros((nlev * C, C), F32)
    for li, m in enumerate(GLA_LEVELS):
        t = r4 - li * C
        hit = (t >= 0) & (t < C) & (c4 == (t // (2 * m)) * (2 * m) + m - 1)
        sel = jnp.where(hit, 1.0, sel)
    sel = sel.astype(BF16)

    g3 = lax.broadcasted_iota(jnp.int32, (C // GLA_DIAG, GLA_DIAG, C), 0)
    t3 = lax.broadcasted_iota(jnp.int32, (C // GLA_DIAG, GLA_DIAG, C), 1)
    s3 = lax.broadcasted_iota(jnp.int32, (C // GLA_DIAG, GLA_DIAG, C), 2)

    for rows in [slice(r, r + C) for r in range(0, q_ref.shape[0], C)]:
        _gla_chunk(rows, q_ref, k_ref, v_ref, gr_ref, glr_ref, wgh_ref, wgl_ref, bg_ref, gn_ref,
                   o_ref, state_ref, row, col, tri, sel, g3, t3, s3)


def _gla_chunk(rows, q_ref, k_ref, v_ref, gr_ref, glr_ref, wgh_ref, wgl_ref, bg_ref, gn_ref,
               o_ref, state_ref, row, col, tri, sel, g3, t3, s3):
    C = GLA_CHUNK
    g_hi, g_lo = _split_bf16(glr_ref[rows, :])
    w_hi, w_lo = wgh_ref[...], wgl_ref[...]
    z = _dot(g_hi, w_hi) + _dot(g_hi, w_lo) + _dot(g_lo, w_hi) + bg_ref[...]
    log_a = (jnp.minimum(z, 0.0) - jnp.log1p(jnp.exp(-jnp.abs(z)))) * (LOG2_E / GLA_TAU)

    a_hi, a_lo = _split_bf16(log_a)
    bcum = _dot(tri, a_hi) + _dot(tri, a_lo)

    ref_all = _dot(sel, bcum.astype(BF16))

    for h in range(GLA_HEADS):
        ks = slice(h * GLA_DK, (h + 1) * GLA_DK)
        vs = slice(h * GLA_DV, (h + 1) * GLA_DV)
        q = q_ref[rows, ks]
        k = k_ref[rows, ks]
        vb = v_ref[rows, vs].astype(BF16)
        b = bcum[:, ks]
        b_last = b[C - 1:C, :]
        state = state_ref[h]

        o = _dot((q * jnp.exp2(b)).astype(BF16), state.astype(BF16))

        scores = jnp.zeros((C, C), F32)
        for li, m in enumerate(GLA_LEVELS):
            ref = ref_all[li * C:(li + 1) * C, ks]
            upper = (row & m) != 0
            qm = q * jnp.exp2(jnp.where(upper, b - ref, NEG_BIG))
            km = k * jnp.exp2(jnp.where(upper, NEG_BIG, ref - b))
            part = _dot_nt(qm.astype(BF16), km.astype(BF16))
            same = (row // (2 * m)) == (col // (2 * m))
            scores = scores + jnp.where(same, part, 0.0)

        groups = C // GLA_DIAG
        q3 = q.reshape(groups, GLA_DIAG, GLA_DK)
        k3 = k.reshape(groups, GLA_DIAG, GLA_DK)
        b3 = b.reshape(groups, GLA_DIAG, GLA_DK)
        diag = jnp.zeros((groups, GLA_DIAG, C), F32)
        for jj in range(GLA_DIAG):
            term = q3 * k3[:, jj:jj + 1, :] * jnp.exp2(b3 - b3[:, jj:jj + 1, :])
            colsum = jnp.sum(term, axis=-1, keepdims=True)
            hit = (s3 == g3 * GLA_DIAG + jj) & (t3 >= jj)
            diag = jnp.where(hit, colsum, diag)
        scores = scores + diag.reshape(C, C)

        o = o + _dot(scores.astype(BF16), vb)

        kl = (k * jnp.exp2(b_last - b)).astype(BF16)
        decay_col = jnp.transpose(jnp.broadcast_to(jnp.exp2(b_last), (C, GLA_DK)))
        decay = jnp.concatenate([decay_col, decay_col], axis=1)
        state_ref[h] = state * decay + _dot_tn(kl, vb)

        rs = lax.rsqrt(jnp.mean(o * o, axis=-1, keepdims=True) + RMS_EPS)
        gate = gr_ref[rows, vs]
        gate = gate / (1.0 + jnp.exp(-gate))
        o_ref[rows, vs] = (o * rs * gn_ref[...] * gate).astype(o_ref.dtype)


def _gla(proj, glr, wg_pad, bg, gnorm, w_cast):
    s = proj.shape[0]
    C = GLA_CHUNK * GLA_STEP_CHUNKS
    wg_hi, wg_lo = _split_bf16(wg_pad)
    cast_in, cast_out, cast_shape = _cast_rider_specs(w_cast, s // C, lambda c: c)
    return pl.pallas_call(
        _gla_kernel,
        grid=(s // C,),
        in_specs=[pl.BlockSpec((C, GLA_QK), lambda c: (c, COL_GQ // GLA_QK)),
                  pl.BlockSpec((C, GLA_QK), lambda c: (c, COL_GK // GLA_QK)),
                  pl.BlockSpec((C, GLA_WIDTH), lambda c: (c, COL_GV // GLA_WIDTH)),
                  pl.BlockSpec((C, GLA_WIDTH), lambda c: (c, COL_GR // GLA_WIDTH)),
                  pl.BlockSpec((C, LANES), lambda c: (c, 0)),
                  pl.BlockSpec((LANES, GLA_QK), lambda c: (0, 0)),
                  pl.BlockSpec((LANES, GLA_QK), lambda c: (0, 0)),
                  pl.BlockSpec((1, GLA_QK), lambda c: (0, 0)),
                  pl.BlockSpec((1, GLA_DV), lambda c: (0, 0)),
                  cast_in],
        out_specs=[pl.BlockSpec((C, GLA_WIDTH), lambda c: (c, 0)), cast_out],
        out_shape=[jax.ShapeDtypeStruct((s, GLA_WIDTH), BF16), cast_shape],
        scratch_shapes=[pltpu.VMEM((GLA_HEADS, GLA_DK, GLA_DV), F32)],
        compiler_params=_params("arbitrary"),
        name="gla",
    )(proj, proj, proj, proj, glr, wg_hi, wg_lo, bg, gnorm, w_cast)


def _att_kernel(q_ref, k_ref, v_ref, wcast_ref, o_ref, wcast_out_ref,
                q1, q4, q16, k1, k4, k16, v1, v4, v16,
                qs_ref, ks_ref, vs_ref, bias_ref, acc_ref, lse_ref):
    n = pl.program_id(1)
    wcast_out_ref[...] = wcast_ref[...].astype(BF16)
    B = ATT_SPAN
    T = ATT_SUPER
    qd, kd, vd = (q1, q4, q16), (k1, k4, k16), (v1, v4, v16)

    row = lax.broadcasted_iota(jnp.int32, (B, 2 * B), 0)
    col = lax.broadcasted_iota(jnp.int32, (B, 2 * B), 1)
    band = (col >= row) & (col <= row + B)
    bias_ref[0] = jnp.where(band, 0.0, NEG_BIG)
    bias_ref[1] = jnp.where(band & (col >= B), 0.0, NEG_BIG)

    @pl.when(n == 0)
    def _():
        for d, kb, vb in zip(ATT_DILATIONS, kd, vd):
            pitch = T // d + B
            for r in range(d):
                kb[r * pitch:r * pitch + B, :] = jnp.zeros((B, ATT_HD), BF16)
                vb[r * pitch:r * pitch + B, :] = jnp.zeros((B, ATT_HD), BF16)

    @pl.when(n > 0)
    def _():
        for d, kb, vb in zip(ATT_DILATIONS, kd, vd):
            pitch = T // d + B
            for r in range(d):
                kb[r * pitch:r * pitch + B, :] = kb[(r + 1) * pitch - B:(r + 1) * pitch, :]
                vb[r * pitch:r * pitch + B, :] = vb[(r + 1) * pitch - B:(r + 1) * pitch, :]

    assert ATT_DILATIONS == (1, 4, 16)
    for src, stage, bufs, front in ((q_ref, qs_ref, qd, 0), (k_ref, ks_ref, kd, B), (v_ref, vs_ref, vd, B)):
        b1, b4, b16 = bufs
        b1[front:front + T, :] = src[...].astype(BF16)
        L4, L16 = T // 4, T // 16
        for r in range(4):
            val = src[pl.ds(r, L4, stride=4), :]
            stage[r * L4:(r + 1) * L4, :] = val
            lo = r * (L4 + front) + front
            b4[lo:lo + L4, :] = val.astype(BF16)
        for r in range(16):
            val = stage[pl.ds((r % 4) * L4 + r // 4, L16, stride=4), :]
            lo = r * (L16 + front) + front
            b16[lo:lo + L16, :] = val.astype(BF16)

    ones = jnp.ones((2 * B, ATT_HD), BF16)

    for bi, (d, qb, kb, vb) in enumerate(zip(ATT_DILATIONS, qd, kd, vd)):
        nblk = T // (B * d)
        pitch = T // d + B

        def body(it, carry, bi=bi, d=d, qb=qb, kb=kb, vb=vb, nblk=nblk, pitch=pitch):
            r = it // nblk
            blk = it % nblk
            qs = qb[pl.ds(pl.multiple_of(it * B, B), B), :]
            kstart = pl.multiple_of(r * pitch + blk * B, B)
            kcat = kb[pl.ds(kstart, 2 * B), :]
            vcat = vb[pl.ds(kstart, 2 * B), :]
            no_prev = jnp.where((n == 0) & (blk == 0), 1, 0)
            s = _dot_nt(qs, kcat) + bias_ref[no_prev]
            m = jnp.max(s, axis=1, keepdims=True)
            p = jnp.exp2(s - m).astype(BF16)
            pv = _dot(p, jnp.concatenate([vcat, ones], axis=1))
            start = r + blk * B * d
            dst = pl.ds(start, B, stride=d) if d > 1 else pl.ds(pl.multiple_of(start, B), B)
            l = pv[:, ATT_HD:]
            acc_ref[bi, dst, :] = pv[:, :ATT_HD] / l
            lse_ref[bi, dst, :] = m + jnp.log2(l)
            return carry

        lax.fori_loop(0, T // B, body, 0, unroll=16)

    s0, s1, s2 = lse_ref[0], lse_ref[1], lse_ref[2]
    mm = jnp.maximum(jnp.maximum(s0, s1), s2)
    c0, c1, c2 = jnp.exp2(s0 - mm), jnp.exp2(s1 - mm), jnp.exp2(s2 - mm)
    num = c0 * acc_ref[0] + c1 * acc_ref[1] + c2 * acc_ref[2]
    o_ref[...] = (num / (c0 + c1 + c2)).astype(o_ref.dtype)


def _attention(proj, w_cast):
    s = proj.shape[0]
    T = ATT_SUPER
    ns = s // T
    qc, kc, vc = COL_AQ // ATT_HD, COL_AK // ATT_HD, COL_AV // ATT_HD
    blk = (T, ATT_HD)
    nb = len(ATT_DILATIONS)
    q_bufs = [pltpu.VMEM((T, ATT_HD), BF16) for _ in ATT_DILATIONS]
    kv_bufs = [pltpu.VMEM((T + d * ATT_SPAN, ATT_HD), BF16) for d in ATT_DILATIONS]
    cast_in, cast_out, cast_shape = _cast_rider_specs(w_cast, ATT_HEADS * ns, lambda h, n: h * ns + n)
    return pl.pallas_call(
        _att_kernel,
        grid=(ATT_HEADS, ns),
        in_specs=[pl.BlockSpec(blk, lambda h, n: (n, qc + h)),
                  pl.BlockSpec(blk, lambda h, n: (n, kc + h)),
                  pl.BlockSpec(blk, lambda h, n: (n, vc + h)),
                  cast_in],
        out_specs=[pl.BlockSpec(blk, lambda h, n: (n, h)), cast_out],
        out_shape=[jax.ShapeDtypeStruct((s, ATT_WIDTH), BF16), cast_shape],
        scratch_shapes=q_bufs + kv_bufs + kv_bufs
        + [pltpu.VMEM((T, ATT_HD), F32)] * 3
        + [pltpu.VMEM((2, ATT_SPAN, 2 * ATT_SPAN), F32)]
        + [pltpu.VMEM((nb, T, LANES), F32)] * 2,
        compiler_params=_params("arbitrary", "arbitrary"),
        name="dilated_attn",
    )(proj, proj, proj, w_cast)


def _layer_norm(z, g, b):
    mu = jnp.mean(z, axis=-1, keepdims=True)
    zc = z - mu
    var = jnp.mean(zc * zc, axis=-1, keepdims=True)
    return zc * lax.rsqrt(var + LN_EPS) * g + b


def _outproj_kernel(og_ref, oa_ref, wg_ref, wa_ref, x_ref, gate_ref, lg_ref, lb_ref, o_ref):
    y = _dot(og_ref[...], wg_ref[...]) + _dot(oa_ref[...], wa_ref[...])
    z = DEEPNORM_ALPHA * x_ref[...] + (1.0 + gate_ref[...]) * y
    o_ref[...] = _layer_norm(z, lg_ref[...], lb_ref[...])


def _outproj(og, oa, wo, x, gate, ln_g, ln_b):
    s, d = x.shape
    tm = min(s, 512)
    vec = pl.BlockSpec((1, d), lambda i: (0, 0))
    return pl.pallas_call(
        _outproj_kernel,
        grid=(s // tm,),
        in_specs=[pl.BlockSpec((tm, GLA_WIDTH), lambda i: (i, 0)),
                  pl.BlockSpec((tm, ATT_WIDTH), lambda i: (i, 0)),
                  pl.BlockSpec((GLA_WIDTH, d), lambda i: (0, 0)),
                  pl.BlockSpec((ATT_WIDTH, d), lambda i: (GLA_WIDTH // ATT_WIDTH, 0)),
                  pl.BlockSpec((tm, d), lambda i: (i, 0)),
                  vec, vec, vec],
        out_specs=pl.BlockSpec((tm, d), lambda i: (i, 0)),
        out_shape=jax.ShapeDtypeStruct((s, d), F32),
        compiler_params=_params("arbitrary"),
        name="out_proj_ln",
    )(og, oa, wo, wo, x, gate, ln_g, ln_b)


def _ffn_kernel(x_ref, sc_ref, sh_ref, gate_ref, wa_ref, wg_ref, cwa_ref, cwg_ref,
                cba_ref, cbg_ref, wd_ref, lg_ref, lb_ref, o_ref,
                u_ref, ta_ref, tg_ref, *, tm, nf):
    i = pl.program_id(0)
    j = pl.program_id(1)
    H = 8
    acc_ref = o_ref

    @pl.when(j == 0)
    def _():
        u_ref[...] = (x_ref[...] * (1.0 + sc_ref[...]) + sh_ref[...]).astype(BF16)
        acc_ref[...] = jnp.zeros_like(acc_ref)

    @pl.when(i == 0)
    def _():
        ta_ref[j] = jnp.zeros(ta_ref.shape[1:], F32)
        tg_ref[j] = jnp.zeros(tg_ref.shape[1:], F32)

    sizes = list(FFN_SPLIT) if tm == sum(FFN_SPLIT) else [tm]
    blocks = [(sum(sizes[:n]), sizes[n]) for n in range(len(sizes))]

    def up(w_ref, r, R):
        return _dot(u_ref[r:r + R, :], w_ref[...])

    def conv(h, halo, cw_ref, cb_ref):
        ext = jnp.concatenate([halo, h], axis=0)
        h1 = pltpu.roll(ext, 1, axis=0)[H:, :]
        h2 = pltpu.roll(ext, 2, axis=0)[H:, :]
        cw = cw_ref[...]
        return cb_ref[...] + cw[0:1, :] * h2 + cw[1:2, :] * h1 + cw[2:3, :] * h

    def chunk(final):
        halo_a, halo_g = ta_ref[j], tg_ref[j]
        nxt = up(wa_ref, *blocks[0]), up(wg_ref, *blocks[0])
        for n, (r, R) in enumerate(blocks):
            ha, hg = nxt
            if n + 1 < len(blocks):
                nxt = up(wa_ref, *blocks[n + 1]), up(wg_ref, *blocks[n + 1])
            ya = conv(ha, halo_a, cwa_ref, cba_ref)
            yg = conv(hg, halo_g, cwg_ref, cbg_ref)
            halo_a, halo_g = ha[R - H:, :], hg[R - H:, :]
            act = (yg / (1.0 + jnp.exp(-yg)) * ya).astype(BF16)
            y = acc_ref[r:r + R, :] + _dot(act, wd_ref[...])
            if final:
                z = DEEPNORM_ALPHA * x_ref[r:r + R, :] + (1.0 + gate_ref[...]) * y
                y = _layer_norm(z, lg_ref[...], lb_ref[...])
            acc_ref[r:r + R, :] = y
        ta_ref[j] = halo_a
        tg_ref[j] = halo_g

    @pl.when(j < nf - 1)
    def _():
        chunk(False)

    @pl.when(j == nf - 1)
    def _():
        chunk(True)


def _ffn(x, sc, sh, gate, w_up, conv_w, conv_b, w_down, ln_g, ln_b):
    s, d = x.shape
    f = w_down.shape[0]
    tm = min(s, 1024)
    fc = 512
    nf = f // fc
    vec = pl.BlockSpec((1, d), lambda i, j: (0, 0))
    return pl.pallas_call(
        functools.partial(_ffn_kernel, tm=tm, nf=nf),
        grid=(s // tm, nf),
        in_specs=[pl.BlockSpec((tm, d), lambda i, j: (i, 0), pipeline_mode=pl.Buffered(1)),
                  vec, vec, vec,
                  pl.BlockSpec((d, fc), lambda i, j: (0, j)),
                  pl.BlockSpec((d, fc), lambda i, j: (0, j + nf)),
                  pl.BlockSpec((3, fc), lambda i, j: (0, j)),
                  pl.BlockSpec((3, fc), lambda i, j: (0, j + nf)),
                  pl.BlockSpec((1, fc), lambda i, j: (0, j)),
                  pl.BlockSpec((1, fc), lambda i, j: (0, j + nf)),
                  pl.BlockSpec((fc, d), lambda i, j: (j, 0)),
                  vec, vec],
        out_specs=pl.BlockSpec((tm, d), lambda i, j: (i, 0)),
        out_shape=jax.ShapeDtypeStruct((s, d), F32),
        scratch_shapes=[pltpu.VMEM((tm, d), BF16),
                        pltpu.VMEM((nf, 8, fc), F32),
                        pltpu.VMEM((nf, 8, fc), F32)],
        compiler_params=_params("arbitrary", "arbitrary"),
        name="conv_ffn_ln",
    )(x, sc, sh, gate, w_up, w_up, conv_w, conv_w, conv_b, conv_b, w_down, ln_g, ln_b)


def kernel(x, c, positions, w_ada, b_ada, w_in, w_gla_gate, b_gla_gate, gla_norm_g, w_out,
           ln1_g, ln1_b, w_up, conv_w, conv_b, w_down, ln2_g, ln2_b):
    batch, s, d = x.shape
    half = ATT_HD // 2
    inv_freq = ROPE_THETA ** (-jnp.arange(half, dtype=F32) / half)
    invf = jnp.concatenate([inv_freq, inv_freq]).reshape(1, ATT_HD)
    outs = []
    for bi in range(batch):
        xb = x.reshape(s, d) if batch == 1 else x[bi]
        for layer in range(w_in.shape[0]):
            mod = _ada(c[bi].reshape(d, 1), w_ada[layer], b_ada[layer].reshape(1, -1))
            sh1, sc1, g1, sh2, sc2, g2 = [mod[:, k * d:(k + 1) * d] for k in range(6)]
            w_main, w_lr = _wprep(jnp.transpose(w_in[layer]))
            wg_pad = jnp.pad(w_gla_gate[layer], ((0, LANES - GLA_RANK), (0, 0)))
            cos, sin, wo = _rope_tables(positions[bi].reshape(s, 1), invf, w_out[layer])
            proj, glr = _inproj(xb, sc1, sh1, w_main, w_lr, cos, sin)
            og, w_up_b = _gla(proj, glr, wg_pad, b_gla_gate[layer].reshape(1, -1),
                              gla_norm_g[layer].reshape(1, -1), w_up[layer])
            oa, w_down_b = _attention(proj, w_down[layer])
            xb = _outproj(og, oa, wo, xb, g1,
                          ln1_g[layer].reshape(1, -1), ln1_b[layer].reshape(1, -1))
            xb = _ffn(xb, sc2, sh2, g2, w_up_b, conv_w[layer],
                      conv_b[layer].reshape(1, -1), w_down_b,
                      ln2_g[layer].reshape(1, -1), ln2_b[layer].reshape(1, -1))
        outs.append(xb)
    return outs[0].reshape(1, s, d) if batch == 1 else jnp.stack(outs, axis=0)
```
